```python
import functools
import jax, jax.numpy as jnp
from jax import lax
import numpy as np

D_MODEL = 1024
BATCH = 8
SEQ = 2048
DEPTH = 4
DEC_BATCH = 128
DEC_SEQ = 4
PAST_LEN = 2048
PAGE_SIZE = 128

A_GROUPS = 4
A_GROUP_DIM = 128
A_WIDTH = 512
CHUNK = 128
B_HEADS = 8
B_HEAD_DIM = 64
B_WIDTH = 512
D_MIX = 1024
DILATED_CONFIGS = ((128, 1), (512, 4), (2048, 16))
SPAN = 128
MAX_WINDOW = 2048
BAND_BLOCK = 128
REL_BUCKETS = 32
REL_MAX_DIST = 2048
ATTN_SCALE = 0.125
EPS = 1e-6
PROJ_SPLITS = (512, 512, 512, 512, 512, 512, 512)
D_IN = 3584

kernel_name = "hymba_gmlp_dilated_swa_step"


def rmsnorm(x, w):
    xf = x.astype(jnp.float32)
    y = xf * lax.rsqrt(jnp.mean(xf * xf, axis=-1, keepdims=True) + EPS)
    return (y * w.astype(jnp.float32)).astype(x.dtype)


def layernorm(x, w, b):
    xf = x.astype(jnp.float32)
    mu = jnp.mean(xf, axis=-1, keepdims=True)
    xc = xf - mu
    y = xc * lax.rsqrt(jnp.mean(xc * xc, axis=-1, keepdims=True) + EPS)
    return (y * w.astype(jnp.float32) + b.astype(jnp.float32)).astype(x.dtype)


def t5_bucket(dist):
    max_exact = REL_BUCKETS // 2
    large = max_exact + (np.log(np.maximum(dist, 1).astype(np.float32) / max_exact)
                         / np.log(REL_MAX_DIST / max_exact) * (REL_BUCKETS - max_exact)).astype(np.int32)
    large = np.minimum(large, REL_BUCKETS - 1)
    return np.where(dist < max_exact, dist, large).astype(np.int32)


def rel_bias_for(rel_bias, d):
    idx = t5_bucket(np.arange(SPAN + 1) * d)
    return rel_bias[idx].astype(jnp.float32)


def in_projection(h, w_in):
    z = jnp.einsum('bsd,de->bse', h, w_in)
    cuts = [int(c) for c in np.cumsum(PROJ_SPLITS)[:-1]]
    return jnp.split(z, cuts, axis=-1)


def spatial_mix(vh, w_s, b_s):
    n = vh.shape[-3]
    w = jnp.where(np.tril(np.ones((n, n), dtype=bool)), w_s[:, :n, :n], 0.0)
    return jnp.einsum('gts,...sgc->...tgc', w, vh) + b_s[:, :n].T[:, :, None]


def prompt_spatial(vn, w_s, b_s):
    B, S, _ = vn.shape
    vh = vn.reshape(B, S // CHUNK, CHUNK, A_GROUPS, A_GROUP_DIM)
    return spatial_mix(vh, w_s, b_s).reshape(B, S, A_WIDTH).astype(vn.dtype)


def sample_spatial(vn, w_s, b_s):
    Bd, T, _ = vn.shape
    vh = vn.reshape(Bd, T, A_GROUPS, A_GROUP_DIM)
    return spatial_mix(vh, w_s, b_s).reshape(Bd, T, A_WIDTH).astype(vn.dtype)


def dilated_band_prompt(q, k, v, bias, d):
    B, S, H, Dh = q.shape
    L = S // d
    nb = -(-L // BAND_BLOCK)
    Lp = nb * BAND_BLOCK

    def residues(x):
        return x.reshape(B, L, d, H, Dh).transpose(0, 2, 1, 3, 4)

    qb = jnp.pad(residues(q), ((0, 0), (0, 0), (0, Lp - L), (0, 0), (0, 0))).reshape(B, d, nb, BAND_BLOCK, H, Dh)
    kp = jnp.pad(residues(k), ((0, 0), (0, 0), (BAND_BLOCK, Lp - L), (0, 0), (0, 0)))
    vp = jnp.pad(residues(v), ((0, 0), (0, 0), (BAND_BLOCK, Lp - L), (0, 0), (0, 0)))

    def band(x):
        prev = x[:, :, :Lp].reshape(B, d, nb, BAND_BLOCK, H, Dh)
        cur = x[:, :, BAND_BLOCK:].reshape(B, d, nb, BAND_BLOCK, H, Dh)
        return jnp.concatenate([prev, cur], axis=3)

    kb, vb = band(kp), band(vp)
    s = jnp.einsum('brnqhc,brnkhc->brnhqk', qb, kb) * ATTN_SCALE
    qi = np.arange(BAND_BLOCK)[:, None]
    kj = np.arange(2 * BAND_BLOCK)[None, :]
    rel = qi + BAND_BLOCK - kj
    rel_ok = (rel >= 0) & (rel <= SPAN)
    start_ok = (np.arange(nb)[:, None] * BAND_BLOCK - BAND_BLOCK + np.arange(2 * BAND_BLOCK)[None, :]) >= 0
    valid = rel_ok[None, :, :] & start_ok[:, None, :]
    s = s + bias[np.clip(rel, 0, SPAN)].transpose(2, 0, 1)[None, None, None]
    s = jnp.where(valid[None, None, :, None], s, -jnp.inf)
    m = jnp.max(s, axis=-1)
    p = jnp.exp(s - m[..., None])
    l = jnp.sum(p, axis=-1)
    num = jnp.einsum('brnhqk,brnkhc->brnqhc', p, vb)

    def back(x):
        x = x.reshape((B, d, Lp) + x.shape[4:])[:, :, :L]
        x = jnp.moveaxis(x, 1, 2)
        return x.reshape((B, S) + x.shape[3:])

    return back(num), back(m.transpose(0, 1, 2, 4, 3)), back(l.transpose(0, 1, 2, 4, 3))


def dilated_gather_sample(q, k_all, v_all, bias, d, wb):
    T = q.shape[1]
    rows = wb + np.arange(T)[:, None] - np.arange(SPAN + 1)[None, :] * d
    valid = rows >= 0
    rows_c = np.clip(rows, 0, None)
    kg = k_all[:, rows_c]
    vg = v_all[:, rows_c]
    s = jnp.einsum('bthc,btkhc->bhtk', q, kg) * ATTN_SCALE + bias.T[None, :, None, :]
    s = jnp.where(valid[None, None], s, -jnp.inf)
    m = jnp.max(s, axis=-1)
    p = jnp.exp(s - m[..., None])
    l = jnp.sum(p, axis=-1)
    num = jnp.einsum('bhtk,btkhc->bthc', p, vg)
    return num, m.transpose(0, 2, 1), l.transpose(0, 2, 1)


def combine_dilations(parts):
    nums = jnp.stack([p[0] for p in parts])
    ms = jnp.stack([p[1] for p in parts])
    ls = jnp.stack([p[2] for p in parts])
    w = jnp.exp(ms - jnp.max(ms, axis=0))
    den = jnp.sum(w * ls, axis=0)
    return jnp.sum(w[..., None] * nums, axis=0) / den[..., None]


def prompt_attention(q, k, v, biases):
    qf, kf, vf = q.astype(jnp.float32), k.astype(jnp.float32), v.astype(jnp.float32)
    parts = [dilated_band_prompt(qf, kf, vf, b, d) for b, (_, d) in zip(biases, DILATED_CONFIGS)]
    return combine_dilations(parts)


def sample_attention(q, k, v, k_buf, v_buf, biases):
    wb = k_buf.shape[1]
    k_all = jnp.concatenate([k_buf.astype(jnp.float32), k.astype(jnp.float32)], axis=1)
    v_all = jnp.concatenate([v_buf.astype(jnp.float32), v.astype(jnp.float32)], axis=1)
    qf = q.astype(jnp.float32)
    parts = [dilated_gather_sample(qf, k_all, v_all, b, d, wb) for b, (_, d) in zip(biases, DILATED_CONFIGS)]
    return combine_dilations(parts)


def mixer_layer(x, norm_w, w_in, ln_v_w, ln_v_b, w_s, b_s, g_a, g_b, w_out, attn_fn, spatial_fn):
    h = rmsnorm(x, norm_w)
    a_u, a_v, a_z, q, k, v, b_z = in_projection(h, w_in)
    vn = layernorm(a_v, ln_v_w, ln_v_b)
    a_out = a_u * spatial_fn(vn, w_s, b_s)
    hs = x.shape[:2] + (B_HEADS, B_HEAD_DIM)
    q, k, v = q.reshape(hs), k.reshape(hs), v.reshape(hs)
    b_out = attn_fn(q, k, v).reshape(x.shape[:2] + (B_WIDTH,)).astype(x.dtype)
    y_a = rmsnorm(a_out * jax.nn.silu(a_z), g_a)
    y_b = rmsnorm(b_out * jax.nn.silu(b_z), g_b)
    y = jnp.einsum('bse,ed->bsd', jnp.concatenate([y_a, y_b], axis=-1), w_out)
    return x + y, k, v, vn


def setup_inputs(seed: int = 0) -> dict:
    key = jax.random.key(seed)
    ks = jax.random.split(key, 16)
    wb = min(MAX_WINDOW, PAST_LEN)
    f32 = jnp.float32
    return {
        "x_prompt": jax.random.normal(ks[0], (BATCH, SEQ, D_MODEL), f32),
        "x_sample": jax.random.normal(ks[1], (DEC_BATCH, DEC_SEQ, D_MODEL), f32),
        "cache_k": jax.random.normal(ks[2], (DEPTH, DEC_BATCH, wb, B_HEADS, B_HEAD_DIM), f32),
        "cache_v": jax.random.normal(ks[3], (DEPTH, DEC_BATCH, wb, B_HEADS, B_HEAD_DIM), f32),
        "norm_w": 1.0 + 0.02 * jax.random.normal(ks[4], (DEPTH, D_MODEL), f32),
        "w_in": jax.random.normal(ks[5], (DEPTH, D_MODEL, D_IN), f32) * D_MODEL ** -0.5,
        "ln_v_w": 1.0 + 0.02 * jax.random.normal(ks[6], (DEPTH, A_WIDTH), f32),
        "ln_v_b": 0.02 * jax.random.normal(ks[7], (DEPTH, A_WIDTH), f32),
        "w_spatial": jax.random.normal(ks[8], (DEPTH, A_GROUPS, CHUNK, CHUNK), f32) * CHUNK ** -0.5,
        "b_spatial": 1.0 + 0.1 * jax.random.normal(ks[9], (DEPTH, A_GROUPS, CHUNK), f32),
        "rel_bias": 0.5 * jax.random.normal(ks[10], (REL_BUCKETS, B_HEADS), f32),
        "out_norm_a": 1.0 + 0.02 * jax.random.normal(ks[11], (DEPTH, A_WIDTH), f32),
        "out_norm_b": 1.0 + 0.02 * jax.random.normal(ks[12], (DEPTH, B_WIDTH), f32),
        "w_out": jax.random.normal(ks[13], (DEPTH, D_MIX, D_MODEL), f32) * D_MIX ** -0.5,
        "final_norm_w": 1.0 + 0.02 * jax.random.normal(ks[14], (D_MODEL,), f32),
    }


def reference(x_prompt, x_sample, cache_k, cache_v, norm_w, w_in, ln_v_w, ln_v_b, w_spatial,
              b_spatial, rel_bias, out_norm_a, out_norm_b, w_out, final_norm_w):
    biases = [rel_bias_for(rel_bias, d) for _, d in DILATED_CONFIGS]
    keep = min(MAX_WINDOW, x_prompt.shape[1])
    xp, xs = x_prompt, x_sample
    pk, pv, sk, sv, sc = [], [], [], [], []
    for l in range(DEPTH):
        lw = (norm_w[l], w_in[l], ln_v_w[l], ln_v_b[l], w_spatial[l], b_spatial[l],
              out_norm_a[l], out_norm_b[l], w_out[l])
        xp, k_p, v_p, _ = mixer_layer(xp, *lw,
                                      attn_fn=functools.partial(prompt_attention, biases=biases),
                                      spatial_fn=prompt_spatial)
        xs, k_s, v_s, vn_s = mixer_layer(xs, *lw,
                                         attn_fn=functools.partial(sample_attention, k_buf=cache_k[l],
                                                                   v_buf=cache_v[l], biases=biases),
                                         spatial_fn=sample_spatial)
        pk.append(k_p[:, -keep:])
        pv.append(v_p[:, -keep:])
        sk.append(k_s)
        sv.append(v_s)
        sc.append(vn_s)
    y_prompt = rmsnorm(xp, final_norm_w)
    y_sample = rmsnorm(xs, final_norm_w)
    return (y_prompt, y_sample, jnp.stack(pk), jnp.stack(pv), jnp.stack(sk), jnp.stack(sv), jnp.stack(sc))
```

```python
import functools

import numpy as np
import jax
import jax.numpy as jnp
from jax import lax
from jax.experimental import pallas as pl
from jax.experimental.pallas import tpu as pltpu

D_MODEL = 1024
DEPTH = 4
WIDTH = 512
N_GROUPS = 4
N_HEADS = 8
HEAD_DIM = 64
N_PAIRS = 4
CHUNK = 128
SPAN = 128
REL_BUCKETS = 32
REL_MAX_DIST = 2048
ATTN_SCALE = 0.125
EPS = 1e-6
NEG = -1e30
N_CLASS = 16
BLK = 128
TOKEN_TILE = 512
VMEM_LIMIT = 56 * 1024 * 1024

F32 = jnp.float32
BF16 = jnp.bfloat16


def _t5_bucket(dist):
    max_exact = REL_BUCKETS // 2
    large = max_exact + (np.log(np.maximum(dist, 1).astype(np.float32) / max_exact)
                         / np.log(REL_MAX_DIST / max_exact) * (REL_BUCKETS - max_exact)).astype(np.int32)
    large = np.minimum(large, REL_BUCKETS - 1)
    return np.where(dist < max_exact, dist, large).astype(np.int32)


def _rms(x, w):
    return x * lax.rsqrt(jnp.mean(x * x, axis=-1, keepdims=True) + EPS) * w


def _silu(z):
    return z / (1.0 + jnp.exp(-z))


def _inproj_common(x_ref, nw_ref, win_ref, lnw_ref, lnb_ref, mix_ref, mixb_ref, ga_ref):
    tm = x_ref.shape[0]
    chunk = mix_ref.shape[1]
    nch = tm // chunk
    h = _rms(x_ref[...], nw_ref[...]).astype(BF16)

    def proj(j):
        return jnp.dot(h, win_ref[:, j * WIDTH:(j + 1) * WIDTH], preferred_element_type=F32)

    a_v = proj(1)
    mu = jnp.mean(a_v, axis=-1, keepdims=True)
    xc = a_v - mu
    vn = xc * lax.rsqrt(jnp.mean(xc * xc, axis=-1, keepdims=True) + EPS) * lnw_ref[...] + lnb_ref[...]
    vnb = vn.astype(BF16)
    cols = []
    for g in range(N_GROUPS):
        vg = vnb[:, g * 128:(g + 1) * 128]
        if nch > 1:
            vg = jnp.concatenate([vg[ci * chunk:(ci + 1) * chunk] for ci in range(nch)], axis=1)
        r = jnp.dot(mix_ref[g], vg, preferred_element_type=F32) + mixb_ref[g]
        if nch > 1:
            r = jnp.concatenate([r[:, ci * 128:(ci + 1) * 128] for ci in range(nch)], axis=0)
        cols.append(r)
    sp = jnp.concatenate(cols, axis=1)
    a_out = proj(0) * sp
    ya = _rms(a_out * _silu(proj(2)), ga_ref[...]).astype(BF16)
    q = proj(3) * ATTN_SCALE
    return ya, vn, q, proj(4), proj(5), proj(6)


def _inproj_prompt_kernel(x_ref, nw_ref, win_ref, lnw_ref, lnb_ref, mix_ref, mixb_ref, ga_ref,
                          ya_ref, q_ref, k_ref, v_ref, kt_ref, vt_ref, bz_ref):
    tm = x_ref.shape[0]
    ya, _, q, k, v, bz = _inproj_common(x_ref, nw_ref, win_ref, lnw_ref, lnb_ref, mix_ref, mixb_ref, ga_ref)
    ya_ref[...] = ya
    bz_ref[...] = bz
    for hp in range(N_PAIRS):
        sl = slice(hp * 128, (hp + 1) * 128)
        q_ref[hp] = q[:, sl]
        k_ref[hp] = k[:, sl]
        v_ref[hp] = v[:, sl]
    kt_ref[...] = k.T.reshape(N_HEADS, HEAD_DIM, tm)
    vt_ref[...] = v.T.reshape(N_HEADS, HEAD_DIM, tm)


def _inproj_sample_kernel(x_ref, nw_ref, win_ref, lnw_ref, lnb_ref, mix_ref, mixb_ref, ga_ref,
                          ya_ref, q_ref, k_ref, v_ref, vn_ref, bz_ref):
    ya, vn, q, k, v, bz = _inproj_common(x_ref, nw_ref, win_ref, lnw_ref, lnb_ref, mix_ref, mixb_ref, ga_ref)
    ya_ref[...] = ya
    q_ref[...] = q
    k_ref[...] = k
    v_ref[...] = v
    vn_ref[...] = vn
    bz_ref[...] = bz


def _weight_specs(chunk):
    full = lambda shape: pl.BlockSpec(shape, lambda *_: (0,) * len(shape))
    return [full((1, D_MODEL)), full((D_MODEL, 7 * WIDTH)), full((1, WIDTH)), full((1, WIDTH)),
            full((N_GROUPS, chunk, chunk)), full((N_GROUPS, chunk, 1)), full((1, WIDTH))]


def _inproj_prompt(x, weights):
    b, s, _ = x.shape
    tm = TOKEN_TILE
    tok = lambda w: pl.BlockSpec((None, tm, w), lambda i, j: (i, j, 0))
    hp_spec = pl.BlockSpec((None, N_PAIRS, tm, 128), lambda i, j: (i, 0, j, 0))
    t_spec = pl.BlockSpec((None, N_HEADS, HEAD_DIM, tm), lambda i, j: (i, 0, 0, j))
    hp_shape = jax.ShapeDtypeStruct((b, N_PAIRS, s, 128), F32)
    t_shape = jax.ShapeDtypeStruct((b, N_HEADS, HEAD_DIM, s), F32)
    return pl.pallas_call(
        _inproj_prompt_kernel,
        grid=(b, s // tm),
        in_specs=[tok(D_MODEL)] + _weight_specs(CHUNK),
        out_specs=[tok(WIDTH), hp_spec, hp_spec, hp_spec, t_spec, t_spec, tok(WIDTH)],
        out_shape=[jax.ShapeDtypeStruct((b, s, WIDTH), BF16), hp_shape, hp_shape, hp_shape,
                   t_shape, t_shape, jax.ShapeDtypeStruct((b, s, WIDTH), F32)],
        compiler_params=pltpu.CompilerParams(
            dimension_semantics=("parallel", "parallel"), vmem_limit_bytes=VMEM_LIMIT),
        name="inproj_prompt",
    )(x, *weights)


def _inproj_sample(x, weights):
    n = x.shape[0]
    full = lambda w: pl.BlockSpec((n, w), lambda i: (0, 0))
    sd = lambda dt: jax.ShapeDtypeStruct((n, WIDTH), dt)
    return pl.pallas_call(
        _inproj_sample_kernel,
        grid=(1,),
        in_specs=[full(D_MODEL)] + _weight_specs(n),
        out_specs=[full(WIDTH)] * 6,
        out_shape=[sd(BF16), sd(F32), sd(F32), sd(F32), sd(F32), sd(F32)],
        compiler_params=pltpu.CompilerParams(
            dimension_semantics=("arbitrary",), vmem_limit_bytes=VMEM_LIMIT),
        name="inproj_sample",
    )(x, *weights)


def _outproj_kernel(x_ref, ya_ref, o_ref, bz_ref, gb_ref, wout_ref, fw_ref, y_ref, *, final):
    o = jnp.concatenate([o_ref[hp] for hp in range(N_PAIRS)], axis=1)
    yb = _rms(o * _silu(bz_ref[...]), gb_ref[...]).astype(BF16)
    ycat = jnp.concatenate([ya_ref[...], yb], axis=1)
    y = x_ref[...] + jnp.dot(ycat, wout_ref[...], preferred_element_type=F32)
    if final:
        y = _rms(y, fw_ref[...])
    y_ref[...] = y


def _outproj(x, ya, o_hp, bz, gb, wout, fw, final):
    b, s, _ = x.shape
    tm = min(TOKEN_TILE, s)
    tok = lambda w: pl.BlockSpec((None, tm, w), lambda i, j: (i, j, 0))
    full = lambda shape: pl.BlockSpec(shape, lambda *_: (0,) * len(shape))
    return pl.pallas_call(
        functools.partial(_outproj_kernel, final=final),
        grid=(b, s // tm),
        in_specs=[tok(D_MODEL), tok(WIDTH),
                  pl.BlockSpec((None, N_PAIRS, tm, 128), lambda i, j: (i, 0, j, 0)),
                  tok(WIDTH), full((1, WIDTH)), full((2 * WIDTH, D_MODEL)), full((1, D_MODEL))],
        out_specs=tok(D_MODEL),
        out_shape=jax.ShapeDtypeStruct((b, s, D_MODEL), F32),
        compiler_params=pltpu.CompilerParams(
            dimension_semantics=("parallel", "parallel"), vmem_limit_bytes=VMEM_LIMIT),
        name="outproj",
    )(x, ya, o_hp, bz, gb, wout, fw)


def _prompt_bias_tables(rel_bias):
    qi = np.arange(BLK)[:, None]
    kj = np.arange(2 * BLK)[None, :]
    part, kk = kj // BLK, kj % BLK
    d1 = 16 * (8 * (1 - part) + qi % 8 - kk % 8) + (qi // 8 - kk // 8)
    ok1 = (d1 >= 0) & (d1 <= SPAN)
    j4 = 4 * (32 * (1 - part) + qi % 32 - kk % 32) + (qi // 32 - kk // 32)
    ok4 = (j4 >= 0) & (j4 <= SPAN)
    j16 = qi - kk + 0 * part
    ok16 = (part == 1) & (j16 >= 0)
    dist = np.stack([d1, 4 * j4, 16 * j16])
    ok = np.stack([ok1, ok4, ok16])
    idx = _t5_bucket(np.clip(dist, 0, None))
    tab = jnp.where(ok[..., None], rel_bias[idx].astype(F32), NEG)
    return jnp.transpose(tab, (0, 3, 1, 2))


def _attn_prompt_kernel(q_ref, k_ref, v_ref, bias_ref, o_ref,
                        qs, ks, vs, m_scr, l_scr, acc_scr, os):
    nrow = q_ref.shape[0] // N_CLASS
    for c in range(N_CLASS):
        qs[c] = q_ref[pl.ds(c, nrow, stride=N_CLASS), :]
        ks[c] = k_ref[pl.ds(c, nrow, stride=N_CLASS), :]
        vs[c] = v_ref[pl.ds(c, nrow, stride=N_CLASS), :]

    lane = lax.broadcasted_iota(jnp.int32, (BLK, 128), 1)
    first_head = lane < HEAD_DIM

    def gather(ref, pieces, lead=()):
        return jnp.concatenate([ref[lead + (c, pl.ds(r0, nr), slice(None))] for c, r0, nr in pieces], axis=0)

    def scatter(ref, pieces, val, lead=()):
        off = 0
        for c, r0, nr in pieces:
            ref[lead + (c, pl.ds(r0, nr), slice(None))] = val[off:off + nr]
            off += nr

    def unit(cfg, q_pieces, prev_pieces, mode):
        k_pieces = prev_pieces + q_pieces
        nk = BLK * (2 if prev_pieces else 1)
        q = gather(qs, q_pieces)
        k = gather(ks, k_pieces).astype(BF16)
        v = gather(vs, k_pieces).astype(BF16)
        vext = jnp.concatenate([v, jnp.ones_like(v)], axis=1)
        pvs, alphas, ls = [], [], []
        for h in range(2):
            sel = first_head if h == 0 else jnp.logical_not(first_head)
            qh = jnp.where(sel, q, 0.0).astype(BF16)
            s = lax.dot_general(qh, k, (((1,), (1,)), ((), ())), preferred_element_type=F32)
            s = s + bias_ref[cfg, h, :, 2 * BLK - nk:]
            m_new = jnp.broadcast_to(jnp.max(s, axis=1, keepdims=True), (BLK, 128))
            if mode != "init":
                m_old = gather(m_scr, q_pieces, (h,))
                m_new = jnp.maximum(m_old, m_new)
                alpha = jnp.exp(m_old - m_new)
            mm = m_new if nk == BLK else jnp.concatenate([m_new, m_new], axis=1)
            p = jnp.exp(s - mm).astype(BF16)
            pv = jnp.dot(p, vext, preferred_element_type=F32)
            l_new = pv[:, 128:]
            if mode != "init":
                l_new = alpha * gather(l_scr, q_pieces, (h,)) + l_new
                alphas.append(alpha)
            if mode != "final":
                scatter(m_scr, q_pieces, m_new, (h,))
                scatter(l_scr, q_pieces, l_new, (h,))
            pvs.append(pv[:, :128])
            ls.append(l_new)
        acc = jnp.where(first_head, pvs[0], pvs[1])
        if mode != "init":
            acc = jnp.where(first_head, alphas[0], alphas[1]) * gather(acc_scr, q_pieces) + acc
        if mode == "final":
            scatter(os, q_pieces, acc / jnp.where(first_head, ls[0], ls[1]))
        else:
            scatter(acc_scr, q_pieces, acc)

    for c in range(N_CLASS):
        unit(2, [(c, 0, BLK)], [], "init")

    for r in range(4):
        cls = [r + 4 * a for a in range(4)]
        unit(1, [(c, 0, 32) for c in cls], [], "update")

        def body4(i, carry, cls=cls):
            n0 = pl.multiple_of(i * 32, 32)
            p0 = pl.multiple_of(n0 - 32, 32)
            unit(1, [(c, n0, 32) for c in cls], [(c, p0, 32) for c in cls], "update")
            return carry

        lax.fori_loop(1, nrow // 32, body4, 0)

    allc = list(range(N_CLASS))
    unit(0, [(c, 0, 8) for c in allc], [], "final")

    def body1(i, carry):
        n0 = pl.multiple_of(i * 8, 8)
        p0 = pl.multiple_of(n0 - 8, 8)
        unit(0, [(c, n0, 8) for c in allc], [(c, p0, 8) for c in allc], "final")
        return carry

    lax.fori_loop(1, nrow // 8, body1, 0)

    for c in range(N_CLASS):
        o_ref[pl.ds(c, nrow, stride=N_CLASS), :] = os[c]


def _attn_prompt(q_hp, k_hp, v_hp, bias_tab):
    b, _, s, _ = q_hp.shape
    nrow = s // N_CLASS
    blk = pl.BlockSpec((None, None, s, 128), lambda i, j: (i, j, 0, 0))
    cls = lambda lead: pltpu.VMEM(lead + (N_CLASS, nrow, 128), F32)
    return pl.pallas_call(
        _attn_prompt_kernel,
        grid=(b, N_PAIRS),
        in_specs=[blk, blk, blk,
                  pl.BlockSpec((3, 2, BLK, 2 * BLK), lambda i, j: (0, j, 0, 0))],
        out_specs=blk,
        out_shape=jax.ShapeDtypeStruct(q_hp.shape, F32),
        scratch_shapes=[cls(()), cls(()), cls(()), cls((2,)), cls((2,)), cls(()), cls(())],
        compiler_params=pltpu.CompilerParams(
            dimension_semantics=("parallel", "parallel"), vmem_limit_bytes=VMEM_LIMIT),
        name="attn_prompt",
    )(q_hp, k_hp, v_hp, bias_tab)


SB = 2
N_NEW = 4
ROWS = SB * N_NEW
WB = 2048
N_TILES = WB // 128
NPOS = WB + 128
FAR_TILES = 12
NEAR_TILES = 3


def _sample_tables(rel_bias):
    r = np.arange(ROWS)[:, None]
    p = np.arange(NPOS)[None, :]
    i = r % N_NEW
    dist = np.where(p < WB, WB + i - p, i - (p - WB))
    ok = (dist >= 0) & ((p < WB) | (p - WB < N_NEW))
    mult = ((dist <= 128).astype(np.int32) + ((dist % 4 == 0) & (dist <= 512))
            + ((dist % 16 == 0) & (dist <= 2048))) * ok
    idx = _t5_bucket(np.clip(dist, 0, None))
    tab = jnp.where((mult > 0)[..., None], rel_bias[idx].astype(F32), NEG)
    return jnp.transpose(tab, (2, 0, 1)), jnp.asarray(mult, F32)


def _sample_selectors():
    lane = np.arange(128)
    sel = np.zeros((ROWS, SB * 6 * 128), np.float32)
    new = np.zeros((ROWS, SB * 128), np.float32)
    for bl in range(SB):
        for i in range(N_NEW):
            row = bl * N_NEW + i
            base = bl * 6 * 128
            sel[row, base + lane[lane % 16 == i]] = 1.0
            sel[row, base + 128 + lane[lane % 4 == i]] = 1.0
            sel[row, base + (2 + i) * 128: base + (3 + i) * 128] = 1.0
            new[row, bl * 128 + i] = 1.0
    return jnp.asarray(sel), jnp.asarray(new)


def _attn_sample_kernel(q_ref, k_ref, v_ref, kt_ref, vt_ref, sel_ref, new_ref, b_ref, w_ref, o_ref):
    tn = (((0,), (0,)), ((), ()))
    nt = (((1,), (1,)), ((), ()))
    pats = lax.dot_general(q_ref[...], sel_ref[...], tn, preferred_element_type=F32)
    ktn = lax.dot_general(k_ref[...], new_ref[...], tn, preferred_element_type=F32)
    vtn = lax.dot_general(v_ref[...], new_ref[...], tn, preferred_element_type=F32)
    row = lax.broadcasted_iota(jnp.int32, (1, ROWS, 128), 1)
    shape3 = (N_HEADS, ROWS, 128)

    def tile3(x2d, j):
        return x2d[:, j * 128:(j + 1) * 128].reshape(N_HEADS, HEAD_DIM, 128)

    per_b = []
    for bl in range(SB):
        pat16, pat4 = tile3(pats, bl * 6), tile3(pats, bl * 6 + 1)
        tok = [tile3(pats, bl * 6 + 2 + i) for i in range(N_NEW)]
        tiles = []
        for j in range(FAR_TILES + NEAR_TILES):
            pat = pat16 if j < FAR_TILES else pat4
            kt = kt_ref[0, bl, :, :, j * 128:(j + 1) * 128]
            tiles.append(jnp.broadcast_to(jnp.sum(kt * pat, axis=1, keepdims=True), shape3))
        for src in (kt_ref[0, bl, :, :, WB - 128:], tile3(ktn, bl)):
            t = jnp.zeros(shape3, F32)
            for i in range(N_NEW):
                s_i = jnp.sum(src * tok[i], axis=1, keepdims=True)
                t = jnp.where(row % N_NEW == i, s_i, t)
            tiles.append(t)
        per_b.append(tiles)
    ntile = len(per_b[0])
    s_tiles = [jnp.where(row < N_NEW, per_b[0][j], per_b[1][j]) + b_ref[:, :, j * 128:(j + 1) * 128]
               for j in range(ntile)]
    m = s_tiles[0]
    for t in s_tiles[1:]:
        m = jnp.maximum(m, t)
    m = jnp.max(m, axis=2, keepdims=True)
    p_tiles = [jnp.exp(s_tiles[j] - m) * w_ref[:, j * 128:(j + 1) * 128][None] for j in range(ntile)]
    l = p_tiles[0]
    for t in p_tiles[1:]:
        l = l + t
    inv = 1.0 / jnp.sum(l, axis=2, keepdims=True)
    p_tiles = [(t * inv).astype(BF16) for t in p_tiles]

    outs = []
    for bl in range(SB):
        heads = []
        for h in range(N_HEADS):
            p_win = jnp.concatenate([p_tiles[j][h] for j in range(N_TILES)], axis=1)
            o = lax.dot_general(vt_ref[0, bl, h].astype(BF16), p_win, nt, preferred_element_type=F32)
            vn_h = tile3(vtn, bl)[h].astype(BF16)
            o = o + lax.dot_general(vn_h, p_tiles[N_TILES][h], nt, preferred_element_type=F32)
            heads.append(o)
        outs.append(jnp.concatenate(heads, axis=0))
    col = lax.broadcasted_iota(jnp.int32, (WIDTH, ROWS), 1)
    o_ref[...] = jnp.where(col < N_NEW, outs[0], outs[1])


def _attn_sample(q, k, v, kt_all, vt_all, layer, sel, new, btab, wtab):
    n = q.shape[0]
    steps = n // ROWS
    rows = pl.BlockSpec((ROWS, WIDTH), lambda s: (s, 0))
    buf = pl.BlockSpec((1, SB, N_HEADS, HEAD_DIM, WB), lambda s: (layer, s, 0, 0, 0))
    full = lambda shape: pl.BlockSpec(shape, lambda s: (0,) * len(shape))
    return pl.pallas_call(
        _attn_sample_kernel,
        grid=(steps,),
        in_specs=[rows, rows, rows, buf, buf, full(sel.shape), full(new.shape),
                  full(btab.shape), full(wtab.shape)],
        out_specs=pl.BlockSpec((None, WIDTH, ROWS), lambda s: (s, 0, 0)),
        out_shape=jax.ShapeDtypeStruct((steps, WIDTH, ROWS), F32),
        compiler_params=pltpu.CompilerParams(
            dimension_semantics=("parallel",), vmem_limit_bytes=VMEM_LIMIT),
        name="attn_sample",
    )(q, k, v, kt_all, vt_all, sel, new, btab, wtab)


def kernel(x_prompt, x_sample, cache_k, cache_v, norm_w, w_in, ln_v_w, ln_v_b, w_spatial, b_spatial,
           rel_bias, out_norm_a, out_norm_b, w_out, final_norm_w):
    b, s, _ = x_prompt.shape
    nb, nt, _ = x_sample.shape
    n_s = nb * nt
    assert cache_k.shape[2] == WB and nt == N_NEW and s % (N_CLASS * BLK) == 0

    kt_all = jnp.transpose(cache_k, (0, 1, 3, 4, 2))
    vt_all = jnp.transpose(cache_v, (0, 1, 3, 4, 2))

    bias_prompt = _prompt_bias_tables(rel_bias)
    btab, wtab = _sample_tables(rel_bias)
    sel, new = _sample_selectors()

    tril = np.tril(np.ones((CHUNK, CHUNK), np.float32))
    tril_s = np.tril(np.ones((nt, nt), np.float32))
    eye_b = jnp.eye(nb, dtype=F32)

    xp = x_prompt
    xs = x_sample.reshape(1, n_s, D_MODEL)
    pk, pv, sk, sv, sc = [], [], [], [], []
    for l in range(DEPTH):
        win = w_in[l].astype(BF16)
        wout = w_out[l].astype(BF16)
        common = (norm_w[l][None], win, ln_v_w[l][None], ln_v_b[l][None])
        mix_p = (w_spatial[l] * tril).astype(BF16)
        mixb_p = b_spatial[l][:, :, None]
        mix_s = jnp.stack([jnp.kron(eye_b, w_spatial[l][g, :nt, :nt] * tril_s)
                           for g in range(N_GROUPS)]).astype(BF16)
        mixb_s = jnp.tile(b_spatial[l][:, :nt], (1, nb))[:, :, None]
        ga, gb = out_norm_a[l][None], out_norm_b[l][None]
        fw = final_norm_w[None]
        final = l == DEPTH - 1

        ya, q_hp, k_hp, v_hp, kt, vt, bz = _inproj_prompt(xp, common + (mix_p, mixb_p, ga))
        o_hp = _attn_prompt(q_hp, k_hp, v_hp, bias_prompt)
        xp = _outproj(xp, ya, o_hp, bz, gb, wout, fw, final)
        pk.append(kt)
        pv.append(vt)

        ya_s, q_s, k_s, v_s, vn_s, bz_s = _inproj_sample(xs[0], common + (mix_s, mixb_s, ga))
        o_t = _attn_sample(q_s, k_s, v_s, kt_all, vt_all, l, sel, new, btab, wtab)
        o_s = o_t.reshape(n_s // ROWS, N_PAIRS, 128, ROWS).transpose(1, 0, 3, 2).reshape(1, N_PAIRS, n_s, 128)
        xs = _outproj(xs, ya_s[None], o_s, bz_s[None], gb, wout, fw, final)
        sk.append(k_s)
        sv.append(v_s)
        sc.append(vn_s)

    heads = (N_HEADS, HEAD_DIM)
    new_k_prompt = jnp.transpose(jnp.stack(pk), (0, 1, 4, 2, 3))
    new_v_prompt = jnp.transpose(jnp.stack(pv), (0, 1, 4, 2, 3))
    new_k_sample = jnp.stack(sk).reshape((DEPTH, nb, nt) + heads)
    new_v_sample = jnp.stack(sv).reshape((DEPTH, nb, nt) + heads)
    new_vchunk = jnp.stack(sc).reshape(DEPTH, nb, nt, WIDTH)
    return (xp, xs.reshape(nb, nt, D_MODEL), new_k_prompt, new_v_prompt,
            new_k_sample, new_v_sample, new_vchunk)
```

```python
import functools

import numpy as np
import jax
import jax.numpy as jnp
from jax import lax
from jax.experimental import pallas as pl
from jax.experimental.pallas import tpu as pltpu

D_MODEL = 1024
DEPTH = 4
WIDTH = 512
N_GROUPS = 4
N_HEADS = 8
HEAD_DIM = 64
N_PAIRS = 4
CHUNK = 128
SPAN = 128
REL_BUCKETS = 32
REL_MAX_DIST = 2048
ATTN_SCALE = 0.125
LOG2E = 1.4426950408889634
Q_SCALE = ATTN_SCALE * LOG2E
EPS = 1e-6
NEG = -1e30
N_CLASS = 16
BLK = 128
TOKEN_TILE = 512
VMEM_LIMIT = 56 * 1024 * 1024

F32 = jnp.float32
BF16 = jnp.bfloat16


def _t5_bucket(dist):
    max_exact = REL_BUCKETS // 2
    large = max_exact + (np.log(np.maximum(dist, 1).astype(np.float32) / max_exact)
                         / np.log(REL_MAX_DIST / max_exact) * (REL_BUCKETS - max_exact)).astype(np.int32)
    large = np.minimum(large, REL_BUCKETS - 1)
    return np.where(dist < max_exact, dist, large).astype(np.int32)


def _rms(x, w):
    return x * lax.rsqrt(jnp.mean(x * x, axis=-1, keepdims=True) + EPS) * w


def _silu(z):
    return z / (1.0 + jnp.exp(-z))


def _bias_table_kernel(idx_ref, rb_ref, out_ref):
    idx = idx_ref[...]
    for h in range(N_HEADS):
        acc = jnp.full(idx.shape, NEG, F32)
        for bucket in range(REL_BUCKETS):
            acc = jnp.where(idx == bucket, rb_ref[bucket, h] * LOG2E, acc)
        out_ref[h] = acc


def _bias_tables(idx, rel_bias):
    n, r, c = idx.shape
    return pl.pallas_call(
        _bias_table_kernel,
        grid=(n,),
        in_specs=[pl.BlockSpec((None, r, c), lambda i: (i, 0, 0)),
                  pl.BlockSpec(memory_space=pltpu.SMEM)],
        out_specs=pl.BlockSpec((None, N_HEADS, r, c), lambda i: (i, 0, 0, 0)),
        out_shape=jax.ShapeDtypeStruct((n, N_HEADS, r, c), F32),
        compiler_params=pltpu.CompilerParams(dimension_semantics=("parallel",)),
        name="bias_tables",
    )(jnp.asarray(idx, jnp.int32), rel_bias.astype(F32))


def _inproj_common(x_ref, nw_ref, win_ref, lnw_ref, lnb_ref, mix_ref, mixb_ref, ga_ref):
    tm = x_ref.shape[0]
    chunk = mix_ref.shape[1]
    nch = tm // chunk
    h = _rms(x_ref[...], nw_ref[...]).astype(BF16)

    def proj(j):
        return jnp.dot(h, win_ref[:, j * WIDTH:(j + 1) * WIDTH], preferred_element_type=F32)

    a_v = proj(1)
    mu = jnp.mean(a_v, axis=-1, keepdims=True)
    xc = a_v - mu
    vn = xc * lax.rsqrt(jnp.mean(xc * xc, axis=-1, keepdims=True) + EPS) * lnw_ref[...] + lnb_ref[...]
    vnb = vn.astype(BF16)
    cols = []
    for g in range(N_GROUPS):
        vg = vnb[:, g * 128:(g + 1) * 128]
        if nch > 1:
            vg = jnp.concatenate([vg[ci * chunk:(ci + 1) * chunk] for ci in range(nch)], axis=1)
        r = jnp.dot(mix_ref[g], vg, preferred_element_type=F32) + mixb_ref[g]
        if nch > 1:
            r = jnp.concatenate([r[:, ci * 128:(ci + 1) * 128] for ci in range(nch)], axis=0)
        cols.append(r)
    sp = jnp.concatenate(cols, axis=1)
    a_out = proj(0) * sp
    ya = _rms(a_out * _silu(proj(2)), ga_ref[...]).astype(BF16)
    q = proj(3) * Q_SCALE
    return ya, vn, q, proj(4), proj(5), proj(6)


def _inproj_prompt_kernel(x_ref, nw_ref, win_ref, lnw_ref, lnb_ref, mix_ref, mixb_ref, ga_ref,
                          ya_ref, q_ref, k_ref, v_ref, kt_ref, vt_ref, bz_ref):
    tm = x_ref.shape[0]
    ya, _, q, k, v, bz = _inproj_common(x_ref, nw_ref, win_ref, lnw_ref, lnb_ref, mix_ref, mixb_ref, ga_ref)
    ya_ref[...] = ya
    bz_ref[...] = bz.astype(bz_ref.dtype)
    for hp in range(N_PAIRS):
        sl = slice(hp * 128, (hp + 1) * 128)
        q_ref[hp] = q[:, sl]
        k_ref[hp] = k[:, sl]
        v_ref[hp] = v[:, sl]
    kt_ref[...] = k.T.reshape(N_HEADS, HEAD_DIM, tm)
    vt_ref[...] = v.T.reshape(N_HEADS, HEAD_DIM, tm)


def _inproj_sample_kernel(x_ref, nw_ref, win_ref, lnw_ref, lnb_ref, mix_ref, mixb_ref, ga_ref,
                          ya_ref, q_ref, k_ref, v_ref, vn_ref, bz_ref):
    ya, vn, q, k, v, bz = _inproj_common(x_ref, nw_ref, win_ref, lnw_ref, lnb_ref, mix_ref, mixb_ref, ga_ref)
    ya_ref[...] = ya
    q_ref[...] = q
    k_ref[...] = k
    v_ref[...] = v
    vn_ref[...] = vn
    bz_ref[...] = bz.astype(bz_ref.dtype)


def _weight_specs(chunk):
    full = lambda shape: pl.BlockSpec(shape, lambda *_: (0,) * len(shape))
    return [full((1, D_MODEL)), full((D_MODEL, 7 * WIDTH)), full((1, WIDTH)), full((1, WIDTH)),
            full((N_GROUPS, chunk, chunk)), full((N_GROUPS, chunk, 1)), full((1, WIDTH))]


def _inproj_prompt(x, weights, layer, kt_stack, vt_stack):
    b, s, _ = x.shape
    tm = TOKEN_TILE
    tok = lambda w: pl.BlockSpec((None, tm, w), lambda i, j: (i, j, 0))
    hp_spec = pl.BlockSpec((None, N_PAIRS, tm, 128), lambda i, j: (i, 0, j, 0))
    t_spec = pl.BlockSpec((None, None, N_HEADS, HEAD_DIM, tm), lambda i, j: (layer, i, 0, 0, j))
    hp_shape = jax.ShapeDtypeStruct((b, N_PAIRS, s, 128), F32)
    t_shape = jax.ShapeDtypeStruct((DEPTH, b, N_HEADS, HEAD_DIM, s), F32)
    in_specs = [tok(D_MODEL)] + _weight_specs(CHUNK)
    args = (x,) + tuple(weights)
    aliases = {}
    kern = _inproj_prompt_kernel
    if kt_stack is not None:
        n_in = len(args)
        in_specs = in_specs + [pl.BlockSpec(memory_space=pl.ANY)] * 2
        args = args + (kt_stack, vt_stack)
        aliases = {n_in: 4, n_in + 1: 5}
        kern = lambda *refs: _inproj_prompt_kernel(*refs[:n_in], *refs[n_in + 2:])
    return pl.pallas_call(
        kern,
        grid=(b, s // tm),
        in_specs=in_specs,
        out_specs=[tok(WIDTH), hp_spec, hp_spec, hp_spec, t_spec, t_spec, tok(WIDTH)],
        out_shape=[jax.ShapeDtypeStruct((b, s, WIDTH), BF16), hp_shape, hp_shape, hp_shape,
                   t_shape, t_shape, jax.ShapeDtypeStruct((b, s, WIDTH), BF16)],
        input_output_aliases=aliases,
        compiler_params=pltpu.CompilerParams(
            dimension_semantics=("parallel", "parallel"), vmem_limit_bytes=VMEM_LIMIT),
        name="inproj_prompt",
    )(*args)


def _inproj_sample(x, weights):
    n = x.shape[0]
    full = lambda w: pl.BlockSpec((n, w), lambda i: (0, 0))
    sd = lambda dt: jax.ShapeDtypeStruct((n, WIDTH), dt)
    return pl.pallas_call(
        _inproj_sample_kernel,
        grid=(1,),
        in_specs=[full(D_MODEL)] + _weight_specs(n),
        out_specs=[full(WIDTH)] * 6,
        out_shape=[sd(BF16), sd(F32), sd(F32), sd(F32), sd(F32), sd(BF16)],
        compiler_params=pltpu.CompilerParams(
            dimension_semantics=("arbitrary",), vmem_limit_bytes=VMEM_LIMIT),
        name="inproj_sample",
    )(x, *weights)


def _outproj_kernel(x_ref, ya_ref, o_ref, bz_ref, gb_ref, wout_ref, fw_ref, y_ref, *, final):
    o = jnp.concatenate([o_ref[hp] for hp in range(N_PAIRS)], axis=1).astype(F32)
    yb = _rms(o * _silu(bz_ref[...].astype(F32)), gb_ref[...]).astype(BF16)
    ycat = jnp.concatenate([ya_ref[...], yb], axis=1)
    y = x_ref[...] + jnp.dot(ycat, wout_ref[...], preferred_element_type=F32)
    if final:
        y = _rms(y, fw_ref[...])
    y_ref[...] = y


def _outproj(x, ya, o_hp, bz, gb, wout, fw, final):
    b, s, _ = x.shape
    tm = min(TOKEN_TILE, s)
    tok = lambda w: pl.BlockSpec((None, tm, w), lambda i, j: (i, j, 0))
    full = lambda shape: pl.BlockSpec(shape, lambda *_: (0,) * len(shape))
    return pl.pallas_call(
        functools.partial(_outproj_kernel, final=final),
        grid=(b, s // tm),
        in_specs=[tok(D_MODEL), tok(WIDTH),
                  pl.BlockSpec((None, N_PAIRS, tm, 128), lambda i, j: (i, 0, j, 0)),
                  tok(WIDTH), full((1, WIDTH)), full((2 * WIDTH, D_MODEL)), full((1, D_MODEL))],
        out_specs=tok(D_MODEL),
        out_shape=jax.ShapeDtypeStruct((b, s, D_MODEL), F32),
        compiler_params=pltpu.CompilerParams(
            dimension_semantics=("parallel", "parallel"), vmem_limit_bytes=VMEM_LIMIT),
        name="outproj",
    )(x, ya, o_hp, bz, gb, wout, fw)


def _prompt_bucket_index():
    qi = np.arange(BLK)[:, None]
    kj = np.arange(2 * BLK)[None, :]
    part, kk = kj // BLK, kj % BLK
    d1 = 16 * (8 * (1 - part) + qi % 8 - kk % 8) + (qi // 8 - kk // 8)
    ok1 = (d1 >= 0) & (d1 <= SPAN)
    j4 = 4 * (32 * (1 - part) + qi % 32 - kk % 32) + (qi // 32 - kk // 32)
    ok4 = (j4 >= 0) & (j4 <= SPAN)
    j16 = qi - kk + 0 * part
    ok16 = (part == 1) & (j16 >= 0)
    dist = np.stack([d1, 4 * j4, 16 * j16])
    ok = np.stack([ok1, ok4, ok16])
    return np.where(ok, _t5_bucket(np.clip(dist, 0, None)), -1)


def _attn_prompt_kernel(q_ref, k_ref, v_ref, bias_ref, o_ref, qs, ks, vs, part_o, part_l, part_m, onat):
    nrow = q_ref.shape[0] // N_CLASS
    for c in range(N_CLASS):
        qs[c] = q_ref[pl.ds(c, nrow, stride=N_CLASS), :]
        ks[c] = k_ref[pl.ds(c, nrow, stride=N_CLASS), :]
        vs[c] = v_ref[pl.ds(c, nrow, stride=N_CLASS), :]

    lane = lax.broadcasted_iota(jnp.int32, (BLK, 128), 1)
    first_head = lane < HEAD_DIM

    def gather(ref, pieces):
        return jnp.concatenate([ref[c, r0:r0 + nr, :] for c, r0, nr in pieces], axis=0)

    def scatter(ref, cfg, pieces, val):
        off = 0
        for c, r0, nr in pieces:
            ref[cfg, c, r0:r0 + nr, :] = val[off:off + nr]
            off += nr

    def pair(x):
        return jnp.where(first_head, x[:BLK], x[BLK:])

    def unit(cfg, q_pieces, prev_pieces):
        k_pieces = prev_pieces + q_pieces
        nk = BLK * (2 if prev_pieces else 1)
        q = gather(qs, q_pieces)
        q2 = jnp.concatenate([jnp.where(first_head, q, 0.0), jnp.where(first_head, 0.0, q)], axis=0)
        k = gather(ks, k_pieces).astype(BF16)
        v = gather(vs, k_pieces).astype(BF16)
        vext = jnp.concatenate([v, jnp.ones_like(v)], axis=1)
        s = lax.dot_general(q2.astype(BF16), k, (((1,), (1,)), ((), ())), preferred_element_type=F32)
        s = s + bias_ref[cfg, :, 2 * BLK - nk:]
        m = jnp.max(s, axis=1, keepdims=True)
        p = jnp.exp2(s - m).astype(BF16)
        pv = jnp.dot(p, vext, preferred_element_type=F32)
        scatter(part_o, cfg, q_pieces, pair(pv[:, :128]))
        scatter(part_l, cfg, q_pieces, pair(pv[:, 128:]))
        scatter(part_m, cfg, q_pieces, pair(jnp.broadcast_to(m, (2 * BLK, 128))))

    for c in range(N_CLASS):
        unit(2, [(c, 0, BLK)], [])
    for r in range(4):
        cls = [r + 4 * a for a in range(4)]
        for i in range(nrow // 32):
            unit(1, [(c, 32 * i, 32) for c in cls], [(c, 32 * i - 32, 32) for c in cls] if i else [])
    allc = list(range(N_CLASS))
    for i in range(nrow // 8):
        unit(0, [(c, 8 * i, 8) for c in allc], [(c, 8 * i - 8, 8) for c in allc] if i else [])

    for c in range(N_CLASS):
        ms = [part_m[cfg, c] for cfg in range(3)]
        m = jnp.maximum(jnp.maximum(ms[0], ms[1]), ms[2])
        ws = [jnp.exp2(mc - m) for mc in ms]
        den = ws[0] * part_l[0, c] + ws[1] * part_l[1, c] + ws[2] * part_l[2, c]
        num = ws[0] * part_o[0, c] + ws[1] * part_o[1, c] + ws[2] * part_o[2, c]
        onat[pl.ds(c, nrow, stride=N_CLASS), :] = num / den
    o_ref[...] = onat[...].astype(o_ref.dtype)


def _attn_prompt(q_hp, k_hp, v_hp, bias_tab):
    b, _, s, _ = q_hp.shape
    nrow = s // N_CLASS
    blk = pl.BlockSpec((None, None, s, 128), lambda i, j: (i, j, 0, 0))
    cls = lambda lead: pltpu.VMEM(lead + (N_CLASS, nrow, 128), F32)
    return pl.pallas_call(
        _attn_prompt_kernel,
        grid=(b, N_PAIRS),
        in_specs=[blk, blk, blk,
                  pl.BlockSpec((3, None, 2 * BLK, 2 * BLK), lambda i, j: (0, j, 0, 0))],
        out_specs=blk,
        out_shape=jax.ShapeDtypeStruct(q_hp.shape, BF16),
        scratch_shapes=[cls(()), cls(()), cls(()), cls((3,)), cls((3,)), cls((3,)),
                        pltpu.VMEM((s, 128), F32)],
        compiler_params=pltpu.CompilerParams(
            dimension_semantics=("parallel", "parallel"), vmem_limit_bytes=VMEM_LIMIT),
        name="attn_prompt",
    )(q_hp, k_hp, v_hp, bias_tab)


SB = 2
N_NEW = 4
ROWS = SB * N_NEW
WB = 2048
N_TILES = WB // 128
NPOS = WB + 128
FAR_TILES = 12
NEAR_TILES = 3


def _sample_tables():
    r = np.arange(ROWS)[:, None]
    p = np.arange(NPOS)[None, :]
    i = r % N_NEW
    dist = np.where(p < WB, WB + i - p, i - (p - WB))
    ok = (dist >= 0) & ((p < WB) | (p - WB < N_NEW))
    mult = ((dist <= 128).astype(np.int32) + ((dist % 4 == 0) & (dist <= 512))
            + ((dist % 16 == 0) & (dist <= 2048))) * ok
    idx = np.where(mult > 0, _t5_bucket(np.clip(dist, 0, None)), -1)
    return idx[None], jnp.asarray(mult, F32)


def _sample_selectors():
    lane = np.arange(128)
    sel = np.zeros((ROWS, SB * 6 * 128), np.float32)
    new = np.zeros((ROWS, SB * 128), np.float32)
    for bl in range(SB):
        for i in range(N_NEW):
            row = bl * N_NEW + i
            base = bl * 6 * 128
            sel[row, base + lane[lane % 16 == i]] = 1.0
            sel[row, base + 128 + lane[lane % 4 == i]] = 1.0
            sel[row, base + (2 + i) * 128: base + (3 + i) * 128] = 1.0
            new[row, bl * 128 + i] = 1.0
    return jnp.asarray(sel), jnp.asarray(new)


def _attn_sample_kernel(q_ref, k_ref, v_ref, kt_ref, vt_ref, sel_ref, new_ref, b_ref, w_ref, o_ref):
    tn = (((0,), (0,)), ((), ()))
    nt = (((1,), (1,)), ((), ()))
    pats = lax.dot_general(q_ref[...], sel_ref[...], tn, preferred_element_type=F32)
    ktn = lax.dot_general(k_ref[...], new_ref[...], tn, preferred_element_type=F32)
    vtn = lax.dot_general(v_ref[...], new_ref[...], tn, preferred_element_type=F32)
    row = lax.broadcasted_iota(jnp.int32, (1, ROWS, 128), 1)
    shape3 = (N_HEADS, ROWS, 128)

    def tile3(x2d, j):
        return x2d[:, j * 128:(j + 1) * 128].reshape(N_HEADS, HEAD_DIM, 128)

    per_b = []
    for bl in range(SB):
        pat16, pat4 = tile3(pats, bl * 6), tile3(pats, bl * 6 + 1)
        tok = [tile3(pats, bl * 6 + 2 + i) for i in range(N_NEW)]
        tiles = []
        for j in range(FAR_TILES + NEAR_TILES):
            pat = pat16 if j < FAR_TILES else pat4
            kt = kt_ref[0, bl, :, :, j * 128:(j + 1) * 128]
            tiles.append(jnp.broadcast_to(jnp.sum(kt * pat, axis=1, keepdims=True), shape3))
        for src in (kt_ref[0, bl, :, :, WB - 128:], tile3(ktn, bl)):
            t = jnp.zeros(shape3, F32)
            for i in range(N_NEW):
                s_i = jnp.sum(src * tok[i], axis=1, keepdims=True)
                t = jnp.where(row % N_NEW == i, s_i, t)
            tiles.append(t)
        per_b.append(tiles)
    ntile = len(per_b[0])
    s_tiles = [jnp.where(row < N_NEW, per_b[0][j], per_b[1][j]) + b_ref[:, :, j * 128:(j + 1) * 128]
               for j in range(ntile)]
    m = s_tiles[0]
    for t in s_tiles[1:]:
        m = jnp.maximum(m, t)
    m = jnp.max(m, axis=2, keepdims=True)
    p_tiles = [jnp.exp2(s_tiles[j] - m) * w_ref[:, j * 128:(j + 1) * 128][None] for j in range(ntile)]
    l = p_tiles[0]
    for t in p_tiles[1:]:
        l = l + t
    inv = 1.0 / jnp.sum(l, axis=2, keepdims=True)
    p_tiles = [(t * inv).astype(BF16) for t in p_tiles]

    outs = []
    for bl in range(SB):
        heads = []
        for h in range(N_HEADS):
            p_win = jnp.concatenate([p_tiles[j][h] for j in range(N_TILES)], axis=1)
            o = lax.dot_general(vt_ref[0, bl, h].astype(BF16), p_win, nt, preferred_element_type=F32)
            vn_h = tile3(vtn, bl)[h].astype(BF16)
            o = o + lax.dot_general(vn_h, p_tiles[N_TILES][h], nt, preferred_element_type=F32)
            heads.append(o)
        outs.append(jnp.concatenate(heads, axis=0))
    col = lax.broadcasted_iota(jnp.int32, (WIDTH, ROWS), 1)
    o_ref[...] = jnp.where(col < N_NEW, outs[0], outs[1])


def _attn_sample(q, k, v, kt_all, vt_all, layer, sel, new, btab, wtab):
    n = q.shape[0]
    steps = n // ROWS
    rows = pl.BlockSpec((ROWS, WIDTH), lambda s: (s, 0))
    buf = pl.BlockSpec((1, SB, N_HEADS, HEAD_DIM, WB), lambda s: (layer, s, 0, 0, 0))
    full = lambda shape: pl.BlockSpec(shape, lambda s: (0,) * len(shape))
    return pl.pallas_call(
        _attn_sample_kernel,
        grid=(steps,),
        in_specs=[rows, rows, rows, buf, buf, full(sel.shape), full(new.shape),
                  full(btab.shape), full(wtab.shape)],
        out_specs=pl.BlockSpec((None, WIDTH, ROWS), lambda s: (s, 0, 0)),
        out_shape=jax.ShapeDtypeStruct((steps, WIDTH, ROWS), F32),
        compiler_params=pltpu.CompilerParams(
            dimension_semantics=("parallel",), vmem_limit_bytes=VMEM_LIMIT),
        name="attn_sample",
    )(q, k, v, kt_all, vt_all, sel, new, btab, wtab)


def kernel(x_prompt, x_sample, cache_k, cache_v, norm_w, w_in, ln_v_w, ln_v_b, w_spatial, b_spatial,
           rel_bias, out_norm_a, out_norm_b, w_out, final_norm_w):
    b, s, _ = x_prompt.shape
    nb, nt, _ = x_sample.shape
    n_s = nb * nt
    assert cache_k.shape[2] == WB and nt == N_NEW and s % (N_CLASS * BLK) == 0

    kt_all = jnp.transpose(cache_k, (0, 1, 3, 4, 2))
    vt_all = jnp.transpose(cache_v, (0, 1, 3, 4, 2))

    bias_prompt = _bias_tables(_prompt_bucket_index(), rel_bias).reshape(3, N_PAIRS, 2 * BLK, 2 * BLK)
    sample_idx, wtab = _sample_tables()
    btab = _bias_tables(sample_idx, rel_bias)[0]
    sel, new = _sample_selectors()

    tril = np.tril(np.ones((CHUNK, CHUNK), np.float32))
    ridx = np.arange(n_s)
    same_row = (ridx[:, None] // nt == ridx[None, :] // nt) & (ridx[:, None] >= ridx[None, :])

    xp = x_prompt
    xs = x_sample.reshape(1, n_s, D_MODEL)
    kt_stack = vt_stack = None
    sk, sv, sc = [], [], []
    for l in range(DEPTH):
        win = w_in[l].astype(BF16)
        wout = w_out[l].astype(BF16)
        common = (norm_w[l][None], win, ln_v_w[l][None], ln_v_b[l][None])
        mix_p = (w_spatial[l] * tril).astype(BF16)
        mixb_p = b_spatial[l][:, :, None]
        mix_s = jnp.where(same_row, jnp.tile(w_spatial[l][:, :nt, :nt], (1, nb, nb)), 0.0).astype(BF16)
        mixb_s = jnp.tile(b_spatial[l][:, :nt], (1, nb))[:, :, None]
        ga, gb = out_norm_a[l][None], out_norm_b[l][None]
        fw = final_norm_w[None]
        final = l == DEPTH - 1

        ya, q_hp, k_hp, v_hp, kt_stack, vt_stack, bz = _inproj_prompt(
            xp, common + (mix_p, mixb_p, ga), l, kt_stack, vt_stack)
        o_hp = _attn_prompt(q_hp, k_hp, v_hp, bias_prompt)
        xp = _outproj(xp, ya, o_hp, bz, gb, wout, fw, final)

        ya_s, q_s, k_s, v_s, vn_s, bz_s = _inproj_sample(xs[0], common + (mix_s, mixb_s, ga))
        o_t = _attn_sample(q_s, k_s, v_s, kt_all, vt_all, l, sel, new, btab, wtab)
        o_s = o_t.reshape(n_s // ROWS, N_PAIRS, 128, ROWS).transpose(1, 0, 3, 2).reshape(1, N_PAIRS, n_s, 128)
        xs = _outproj(xs, ya_s[None], o_s, bz_s[None], gb, wout, fw, final)
        sk.append(k_s)
        sv.append(v_s)
        sc.append(vn_s)

    heads = (N_HEADS, HEAD_DIM)
    new_k_prompt = jnp.transpose(kt_stack, (0, 1, 4, 2, 3))
    new_v_prompt = jnp.transpose(vt_stack, (0, 1, 4, 2, 3))
    new_k_sample = jnp.stack(sk).reshape((DEPTH, nb, nt) + heads)
    new_v_sample = jnp.stack(sv).reshape((DEPTH, nb, nt) + heads)
    new_vchunk = jnp.stack(sc).reshape(DEPTH, nb, nt, WIDTH)
    return (xp, xs.reshape(nb, nt, D_MODEL), new_k_prompt, new_v_prompt,
            new_k_sample, new_v_sample, new_vchunk)
```

```python
import functools

import numpy as np
import jax
import jax.numpy as jnp
from jax import lax
from jax.experimental import pallas as pl
from jax.experimental.pallas import tpu as pltpu

D_MODEL = 1024
DEPTH = 4
WIDTH = 512
N_GROUPS = 4
N_HEADS = 8
HEAD_DIM = 64
N_PAIRS = 4
CHUNK = 128
SPAN = 128
REL_BUCKETS = 32
REL_MAX_DIST = 2048
ATTN_SCALE = 0.125
LOG2E = 1.4426950408889634
Q_SCALE = ATTN_SCALE * LOG2E
EPS = 1e-6
NEG = -1e30
N_CLASS = 16
BLK = 128
INPROJ_TILE = 128
OUTPROJ_TILE = 512
VMEM_LIMIT = 56 * 1024 * 1024

F32 = jnp.float32
BF16 = jnp.bfloat16


def _t5_bucket(dist):
    max_exact = REL_BUCKETS // 2
    large = max_exact + (np.log(np.maximum(dist, 1).astype(np.float32) / max_exact)
                         / np.log(REL_MAX_DIST / max_exact) * (REL_BUCKETS - max_exact)).astype(np.int32)
    large = np.minimum(large, REL_BUCKETS - 1)
    return np.where(dist < max_exact, dist, large).astype(np.int32)


def _rms(x, w):
    return x * lax.rsqrt(jnp.mean(x * x, axis=-1, keepdims=True) + EPS) * w


def _silu(z):
    return z / (1.0 + jnp.exp(-z))


def _full_spec(shape):
    return pl.BlockSpec(shape, lambda *_: (0,) * len(shape))


def _bias_table_kernel(idx_ref, rb_ref, out_ref):
    idx = idx_ref[...]
    for h in range(N_HEADS):
        acc = jnp.full(idx.shape, NEG, F32)
        for bucket in range(REL_BUCKETS):
            acc = jnp.where(idx == bucket, rb_ref[bucket, h] * LOG2E, acc)
        out_ref[h] = acc


def _bias_tables(idx, rel_bias):
    n, r, c = idx.shape
    return pl.pallas_call(
        _bias_table_kernel,
        grid=(n,),
        in_specs=[pl.BlockSpec((None, r, c), lambda i: (i, 0, 0)),
                  pl.BlockSpec(memory_space=pltpu.SMEM)],
        out_specs=pl.BlockSpec((None, N_HEADS, r, c), lambda i: (i, 0, 0, 0)),
        out_shape=jax.ShapeDtypeStruct((n, N_HEADS, r, c), F32),
        compiler_params=pltpu.CompilerParams(dimension_semantics=("parallel",)),
        name="bias_tables",
    )(jnp.asarray(idx, jnp.int32), rel_bias.astype(F32))


N_NEW = 4
ROWS = 8
WB = 2048
N_TILES = WB // 128
NPOS = WB + 128
FAR_TILES = 12
NEAR_TILES = 3
N_PAT = 2 + N_NEW


def _sample_tables():
    r = np.arange(ROWS)[:, None]
    p = np.arange(NPOS)[None, :]
    i = r % N_NEW
    dist = np.where(p < WB, WB + i - p, i - (p - WB))
    ok = (dist >= 0) & ((p < WB) | (p - WB < N_NEW))
    mult = ((dist <= 128).astype(np.int32) + ((dist % 4 == 0) & (dist <= 512))
            + ((dist % 16 == 0) & (dist <= 2048))) * ok
    idx = np.where(mult > 0, _t5_bucket(np.clip(dist, 0, None)), -1)
    return idx[None], jnp.asarray(mult, F32)


def _sample_selectors():
    lane = np.arange(128)
    sel = np.zeros((ROWS, N_PAT * 128), np.float32)
    new = np.zeros((ROWS, 128), np.float32)
    for i in range(N_NEW):
        sel[i, lane[lane % 16 == i]] = 1.0
        sel[i, 128 + lane[lane % 4 == i]] = 1.0
        sel[i, (2 + i) * 128:(3 + i) * 128] = 1.0
        new[i, i] = 1.0
    return jnp.asarray(sel, BF16), jnp.asarray(new, BF16)


def _attn_sample_body(q, k, v, kt_ref, vt_ref, sel_ref, new_ref, b_ref, w_ref):
    tn = (((0,), (0,)), ((), ()))
    nt = (((1,), (1,)), ((), ()))
    pats = lax.dot_general(q.astype(BF16), sel_ref[...], tn, preferred_element_type=F32)
    ktn = lax.dot_general(k.astype(BF16), new_ref[...], tn, preferred_element_type=F32)
    vtn = lax.dot_general(v.astype(BF16), new_ref[...], tn, preferred_element_type=F32)
    row = lax.broadcasted_iota(jnp.int32, (1, ROWS, 128), 1)
    shape3 = (N_HEADS, ROWS, 128)

    def tile3(x2d, j):
        return x2d[:, j * 128:(j + 1) * 128].reshape(N_HEADS, HEAD_DIM, 128)

    pat16, pat4 = tile3(pats, 0), tile3(pats, 1)
    tok = [tile3(pats, 2 + i) for i in range(N_NEW)]
    s_tiles = []
    for j in range(FAR_TILES + NEAR_TILES):
        pat = pat16 if j < FAR_TILES else pat4
        kt = kt_ref[:, :, j * 128:(j + 1) * 128]
        s_tiles.append(jnp.broadcast_to(jnp.sum(kt * pat, axis=1, keepdims=True), shape3))
    for src in (kt_ref[:, :, WB - 128:], tile3(ktn, 0)):
        t = jnp.zeros(shape3, F32)
        for i in range(N_NEW):
            s_i = jnp.sum(src * tok[i], axis=1, keepdims=True)
            t = jnp.where(row % N_NEW == i, s_i, t)
        s_tiles.append(t)
    ntile = len(s_tiles)
    s_tiles = [s_tiles[j] + b_ref[:, :, j * 128:(j + 1) * 128] for j in range(ntile)]
    m = s_tiles[0]
    for t in s_tiles[1:]:
        m = jnp.maximum(m, t)
    m = jnp.max(m, axis=2, keepdims=True)
    p_tiles = [jnp.exp2(s_tiles[j] - m) * w_ref[:, j * 128:(j + 1) * 128][None] for j in range(ntile)]
    l = p_tiles[0]
    for t in p_tiles[1:]:
        l = l + t
    inv = 1.0 / jnp.sum(l, axis=2, keepdims=True)
    p_tiles = [(t * inv).astype(BF16) for t in p_tiles]

    heads = []
    for h in range(N_HEADS):
        p_win = jnp.concatenate([p_tiles[j][h] for j in range(N_TILES)], axis=1)
        o = lax.dot_general(vt_ref[h].astype(BF16), p_win, nt, preferred_element_type=F32)
        vn_h = tile3(vtn, 0)[h].astype(BF16)
        o = o + lax.dot_general(vn_h, p_tiles[N_TILES][h], nt, preferred_element_type=F32)
        heads.append(o)
    return jnp.concatenate(heads, axis=0)


def _inproj_core(x_ref, nw_ref, win_ref, lnw_ref, lnb_ref, ga_ref, mix_fn):
    h = _rms(x_ref[...], nw_ref[...]).astype(BF16)

    def proj(j):
        return jnp.dot(h, win_ref[:, j * WIDTH:(j + 1) * WIDTH], preferred_element_type=F32)

    a_v = proj(1)
    mu = jnp.mean(a_v, axis=-1, keepdims=True)
    xc = a_v - mu
    vn = xc * lax.rsqrt(jnp.mean(xc * xc, axis=-1, keepdims=True) + EPS) * lnw_ref[...] + lnb_ref[...]
    a_out = proj(0) * mix_fn(vn)
    ya = _rms(a_out * _silu(proj(2)), ga_ref[...]).astype(BF16)
    q = proj(3) * Q_SCALE
    return ya, vn, q, proj(4), proj(5), proj(6)


def _mix_chunks(vn, mix_ref, mixb_ref):
    tm = vn.shape[0]
    chunk = mix_ref.shape[1]
    nch = tm // chunk
    vnb = vn.astype(BF16)
    cols = []
    for g in range(N_GROUPS):
        vg = vnb[:, g * 128:(g + 1) * 128]
        if nch > 1:
            vg = jnp.concatenate([vg[ci * chunk:(ci + 1) * chunk] for ci in range(nch)], axis=1)
        r = jnp.dot(mix_ref[g], vg, preferred_element_type=F32) + mixb_ref[g]
        if nch > 1:
            r = jnp.concatenate([r[:, ci * 128:(ci + 1) * 128] for ci in range(nch)], axis=0)
        cols.append(r)
    return jnp.concatenate(cols, axis=1)


def _mix_new_tokens(vn, coef_ref, mixb_ref):
    cols = []
    for g in range(N_GROUPS):
        vg = vn[:, g * 128:(g + 1) * 128]
        acc = coef_ref[g, 0] * vg + mixb_ref[g]
        for d in range(1, N_NEW):
            acc = acc + coef_ref[g, d] * pltpu.roll(vg, d, axis=0)
        cols.append(acc)
    return jnp.concatenate(cols, axis=1)


N_W = 8
N_S = 9


def _inproj_prompt_kernel(*refs):
    x_ref, nw_ref, win_ref, lnw_ref, lnb_ref, mix_ref, mixb_ref, ga_ref = refs[:N_W]
    qs_ref, ks_ref, vs_ref, ktw_ref, vtw_ref, sel_ref, new_ref, b_ref, w_ref = refs[N_W:N_W + N_S]
    ya_ref, q_ref, k_ref, v_ref, kt_ref, vt_ref, bz_ref, os_ref = refs[-8:]
    tm = x_ref.shape[0]
    ya, _, q, k, v, bz = _inproj_core(
        x_ref, nw_ref, win_ref, lnw_ref, lnb_ref, ga_ref,
        functools.partial(_mix_chunks, mix_ref=mix_ref, mixb_ref=mixb_ref))
    ya_ref[...] = ya
    bz_ref[...] = bz.astype(bz_ref.dtype)
    for hp in range(N_PAIRS):
        sl = slice(hp * 128, (hp + 1) * 128)
        q_ref[hp] = q[:, sl]
        k_ref[hp] = k[:, sl]
        v_ref[hp] = v[:, sl]
    kt_ref[...] = k.T.reshape(N_HEADS, HEAD_DIM, tm)
    vt_ref[...] = v.T.reshape(N_HEADS, HEAD_DIM, tm)
    os_ref[...] = _attn_sample_body(qs_ref[...], ks_ref[...], vs_ref[...], ktw_ref, vtw_ref,
                                    sel_ref, new_ref, b_ref, w_ref)


def _inproj_sample_kernel(x_ref, nw_ref, win_ref, lnw_ref, lnb_ref, coef_ref, mixb_ref, ga_ref,
                          ya_ref, q_ref, k_ref, v_ref, vn_ref, bz_ref):
    ya, vn, q, k, v, bz = _inproj_core(
        x_ref, nw_ref, win_ref, lnw_ref, lnb_ref, ga_ref,
        functools.partial(_mix_new_tokens, coef_ref=coef_ref, mixb_ref=mixb_ref))
    ya_ref[...] = ya
    q_ref[...] = q
    k_ref[...] = k
    v_ref[...] = v
    vn_ref[...] = vn
    bz_ref[...] = bz.astype(bz_ref.dtype)


def _inproj_prompt(x, weights, layer, kt_stack, vt_stack, sample_ops):
    b, s, _ = x.shape
    tm = INPROJ_TILE
    nj = s // tm
    qs, ks, vs, ktw, vtw, sel, new, btab, wtab = sample_ops
    nb = qs.shape[0]
    assert b * nj == nb
    tok = lambda w: pl.BlockSpec((None, tm, w), lambda i, j: (i, j, 0))
    hp_spec = pl.BlockSpec((None, N_PAIRS, tm, 128), lambda i, j: (i, 0, j, 0))
    t_spec = pl.BlockSpec((None, None, N_HEADS, HEAD_DIM, tm), lambda i, j: (layer, i, 0, 0, j))
    hp_shape = jax.ShapeDtypeStruct((b, N_PAIRS, s, 128), F32)
    t_shape = jax.ShapeDtypeStruct((DEPTH, b, N_HEADS, HEAD_DIM, s), F32)
    rows = pl.BlockSpec((None, ROWS, WIDTH), lambda i, j: (i * nj + j, 0, 0))
    buf = pl.BlockSpec((None, None, N_HEADS, HEAD_DIM, WB), lambda i, j: (layer, i * nj + j, 0, 0, 0))
    in_specs = ([tok(D_MODEL), _full_spec((1, D_MODEL)), _full_spec((D_MODEL, 7 * WIDTH)),
                 _full_spec((1, WIDTH)), _full_spec((1, WIDTH)),
                 _full_spec((N_GROUPS, CHUNK, CHUNK)), _full_spec((N_GROUPS, CHUNK, 1)),
                 _full_spec((1, WIDTH))]
                + [rows, rows, rows, buf, buf, _full_spec(sel.shape), _full_spec(new.shape),
                   _full_spec(btab.shape), _full_spec(wtab.shape)])
    args = (x,) + tuple(weights) + tuple(sample_ops)
    assert len(args) == N_W + N_S
    aliases = {}
    if kt_stack is not None:
        in_specs = in_specs + [pl.BlockSpec(memory_space=pl.ANY)] * 2
        aliases = {len(args): 4, len(args) + 1: 5}
        args = args + (kt_stack, vt_stack)
    return pl.pallas_call(
        _inproj_prompt_kernel,
        grid=(b, nj),
        in_specs=in_specs,
        out_specs=[tok(WIDTH), hp_spec, hp_spec, hp_spec, t_spec, t_spec, tok(WIDTH),
                   pl.BlockSpec((None, WIDTH, ROWS), lambda i, j: (i * nj + j, 0, 0))],
        out_shape=[jax.ShapeDtypeStruct((b, s, WIDTH), BF16), hp_shape, hp_shape, hp_shape,
                   t_shape, t_shape, jax.ShapeDtypeStruct((b, s, WIDTH), BF16),
                   jax.ShapeDtypeStruct((nb, WIDTH, ROWS), F32)],
        input_output_aliases=aliases,
        compiler_params=pltpu.CompilerParams(
            dimension_semantics=("parallel", "parallel"), vmem_limit_bytes=VMEM_LIMIT),
        name="inproj_prompt",
    )(*args)


def _inproj_sample(x, weights):
    n = x.shape[0]
    full = lambda w: pl.BlockSpec((n, w), lambda i: (0, 0))
    sd = lambda dt: jax.ShapeDtypeStruct((n, WIDTH), dt)
    return pl.pallas_call(
        _inproj_sample_kernel,
        grid=(1,),
        in_specs=[full(D_MODEL), _full_spec((1, D_MODEL)), _full_spec((D_MODEL, 7 * WIDTH)),
                  _full_spec((1, WIDTH)), _full_spec((1, WIDTH)),
                  _full_spec((N_GROUPS, N_NEW, n, 1)), _full_spec((N_GROUPS, n, 1)),
                  _full_spec((1, WIDTH))],
        out_specs=[full(WIDTH)] * 6,
        out_shape=[sd(BF16), sd(F32), sd(F32), sd(F32), sd(F32), sd(BF16)],
        compiler_params=pltpu.CompilerParams(
            dimension_semantics=("arbitrary",), vmem_limit_bytes=VMEM_LIMIT),
        name="inproj_sample",
    )(x, *weights)


def _outproj_kernel(x_ref, ya_ref, o_ref, bz_ref, gb_ref, wout_ref, fw_ref, y_ref, *, final):
    o = jnp.concatenate([o_ref[hp] for hp in range(N_PAIRS)], axis=1).astype(F32)
    yb = _rms(o * _silu(bz_ref[...].astype(F32)), gb_ref[...]).astype(BF16)
    ycat = jnp.concatenate([ya_ref[...], yb], axis=1)
    y = x_ref[...] + jnp.dot(ycat, wout_ref[...], preferred_element_type=F32)
    if final:
        y = _rms(y, fw_ref[...])
    y_ref[...] = y


def _outproj(x, ya, o_hp, bz, gb, wout, fw, final):
    b, s, _ = x.shape
    tm = min(OUTPROJ_TILE, s)
    tok = lambda w: pl.BlockSpec((None, tm, w), lambda i, j: (i, j, 0))
    return pl.pallas_call(
        functools.partial(_outproj_kernel, final=final),
        grid=(b, s // tm),
        in_specs=[tok(D_MODEL), tok(WIDTH),
                  pl.BlockSpec((None, N_PAIRS, tm, 128), lambda i, j: (i, 0, j, 0)),
                  tok(WIDTH), _full_spec((1, WIDTH)), _full_spec((2 * WIDTH, D_MODEL)),
                  _full_spec((1, D_MODEL))],
        out_specs=tok(D_MODEL),
        out_shape=jax.ShapeDtypeStruct((b, s, D_MODEL), F32),
        compiler_params=pltpu.CompilerParams(
            dimension_semantics=("parallel", "parallel"), vmem_limit_bytes=VMEM_LIMIT),
        name="outproj",
    )(x, ya, o_hp, bz, gb, wout, fw)


def _prompt_bucket_index():
    qi = np.arange(BLK)[:, None]
    kj = np.arange(2 * BLK)[None, :]
    part, kk = kj // BLK, kj % BLK
    d1 = 16 * (8 * (1 - part) + qi % 8 - kk % 8) + (qi // 8 - kk // 8)
    ok1 = (d1 >= 0) & (d1 <= SPAN)
    j4 = 4 * (32 * (1 - part) + qi % 32 - kk % 32) + (qi // 32 - kk // 32)
    ok4 = (j4 >= 0) & (j4 <= SPAN)
    j16 = qi - kk + 0 * part
    ok16 = (part == 1) & (j16 >= 0)
    dist = np.stack([d1, 4 * j4, 16 * j16])
    ok = np.stack([ok1, ok4, ok16])
    return np.where(ok, _t5_bucket(np.clip(dist, 0, None)), -1)


def _attn_prompt_kernel(q_ref, k_ref, v_ref, bias_ref, o_ref, qs, ks, vs, part_o, part_l, part_m, onat):
    nrow = q_ref.shape[0] // N_CLASS
    for c in range(N_CLASS):
        qs[c] = q_ref[pl.ds(c, nrow, stride=N_CLASS), :]
        ks[c] = k_ref[pl.ds(c, nrow, stride=N_CLASS), :]
        vs[c] = v_ref[pl.ds(c, nrow, stride=N_CLASS), :]

    lane = lax.broadcasted_iota(jnp.int32, (BLK, 128), 1)
    first_head = lane < HEAD_DIM

    def gather(ref, pieces):
        return jnp.concatenate([ref[c, r0:r0 + nr, :] for c, r0, nr in pieces], axis=0)

    def scatter(ref, cfg, pieces, val):
        off = 0
        for c, r0, nr in pieces:
            ref[cfg, c, r0:r0 + nr, :] = val[off:off + nr]
            off += nr

    def pair(x):
        return jnp.where(first_head, x[:BLK], x[BLK:])

    def unit(cfg, q_pieces, prev_pieces):
        k_pieces = prev_pieces + q_pieces
        nk = BLK * (2 if prev_pieces else 1)
        q = gather(qs, q_pieces)
        q2 = jnp.concatenate([jnp.where(first_head, q, 0.0), jnp.where(first_head, 0.0, q)], axis=0)
        k = gather(ks, k_pieces).astype(BF16)
        v = gather(vs, k_pieces).astype(BF16)
        vext = jnp.concatenate([v, jnp.ones_like(v)], axis=1)
        s = lax.dot_general(q2.astype(BF16), k, (((1,), (1,)), ((), ())), preferred_element_type=F32)
        s = s + bias_ref[cfg, :, 2 * BLK - nk:]
        m = jnp.max(s, axis=1, keepdims=True)
        p = jnp.exp2(s - m).astype(BF16)
        pv = jnp.dot(p, vext, preferred_element_type=F32)
        scatter(part_o, cfg, q_pieces, pair(pv[:, :128]))
        scatter(part_l, cfg, q_pieces, pair(pv[:, 128:]))
        scatter(part_m, cfg, q_pieces, pair(jnp.broadcast_to(m, (2 * BLK, 128))))

    for c in range(N_CLASS):
        unit(2, [(c, 0, BLK)], [])
    for r in range(4):
        cls = [r + 4 * a for a in range(4)]
        for i in range(nrow // 32):
            unit(1, [(c, 32 * i, 32) for c in cls], [(c, 32 * i - 32, 32) for c in cls] if i else [])
    allc = list(range(N_CLASS))
    for i in range(nrow // 8):
        unit(0, [(c, 8 * i, 8) for c in allc], [(c, 8 * i - 8, 8) for c in allc] if i else [])

    for c in range(N_CLASS):
        ms = [part_m[cfg, c] for cfg in range(3)]
        m = jnp.maximum(jnp.maximum(ms[0], ms[1]), ms[2])
        ws = [jnp.exp2(mc - m) for mc in ms]
        den = ws[0] * part_l[0, c] + ws[1] * part_l[1, c] + ws[2] * part_l[2, c]
        num = ws[0] * part_o[0, c] + ws[1] * part_o[1, c] + ws[2] * part_o[2, c]
        onat[pl.ds(c, nrow, stride=N_CLASS), :] = num / den
    o_ref[...] = onat[...].astype(o_ref.dtype)


def _attn_prompt(q_hp, k_hp, v_hp, bias_tab):
    b, _, s, _ = q_hp.shape
    nrow = s // N_CLASS
    blk = pl.BlockSpec((None, None, s, 128), lambda i, j: (i, j, 0, 0))
    cls = lambda lead: pltpu.VMEM(lead + (N_CLASS, nrow, 128), F32)
    return pl.pallas_call(
        _attn_prompt_kernel,
        grid=(b, N_PAIRS),
        in_specs=[blk, blk, blk,
                  pl.BlockSpec((3, None, 2 * BLK, 2 * BLK), lambda i, j: (0, j, 0, 0))],
        out_specs=blk,
        out_shape=jax.ShapeDtypeStruct(q_hp.shape, BF16),
        scratch_shapes=[cls(()), cls(()), cls(()), cls((3,)), cls((3,)), cls((3,)),
                        pltpu.VMEM((s, 128), F32)],
        compiler_params=pltpu.CompilerParams(
            dimension_semantics=("parallel", "parallel"), vmem_limit_bytes=VMEM_LIMIT),
        name="attn_prompt",
    )(q_hp, k_hp, v_hp, bias_tab)


def _new_token_mix_selectors(nb):
    n = nb * N_NEW
    t = np.arange(n) % N_NEW
    sel_w = np.zeros((N_NEW, N_NEW, N_NEW, n), np.float32)
    sel_b = np.zeros((N_NEW, n), np.float32)
    for r in range(n):
        sel_b[t[r], r] = 1.0
        for d in range(t[r] + 1):
            sel_w[d, t[r], t[r] - d, r] = 1.0
    return jnp.asarray(sel_w), jnp.asarray(sel_b)


def kernel(x_prompt, x_sample, cache_k, cache_v, norm_w, w_in, ln_v_w, ln_v_b, w_spatial, b_spatial,
           rel_bias, out_norm_a, out_norm_b, w_out, final_norm_w):
    b, s, _ = x_prompt.shape
    nb, nt, _ = x_sample.shape
    n_s = nb * nt
    assert cache_k.shape[2] == WB and nt == N_NEW and s % (N_CLASS * BLK) == 0

    kt_all = jnp.transpose(cache_k, (0, 1, 3, 4, 2))
    vt_all = jnp.transpose(cache_v, (0, 1, 3, 4, 2))

    bias_prompt = _bias_tables(_prompt_bucket_index(), rel_bias).reshape(3, N_PAIRS, 2 * BLK, 2 * BLK)
    sample_idx, wtab = _sample_tables()
    btab = _bias_tables(sample_idx, rel_bias)[0]
    sel, new = _sample_selectors()
    sel_w, sel_b = _new_token_mix_selectors(nb)
    hi = lax.Precision.HIGHEST

    tril = np.tril(np.ones((CHUNK, CHUNK), np.float32))
    rows8 = lambda a: jnp.concatenate([a.reshape(nb, nt, WIDTH)] * (ROWS // nt), axis=1)

    xp = x_prompt
    xs = x_sample.reshape(1, n_s, D_MODEL)
    kt_stack = vt_stack = None
    sk, sv, sc = [], [], []
    for l in range(DEPTH):
        win = w_in[l].astype(BF16)
        wout = w_out[l].astype(BF16)
        common = (norm_w[l][None], win, ln_v_w[l][None], ln_v_b[l][None])
        mix_p = (w_spatial[l] * tril).astype(BF16)
        mixb_p = b_spatial[l][:, :, None]
        coef_s = jnp.einsum('gts,dtsr->gdr', w_spatial[l][:, :nt, :nt], sel_w, precision=hi)[..., None]
        mixb_s = jnp.einsum('gt,tr->gr', b_spatial[l][:, :nt], sel_b, precision=hi)[..., None]
        ga, gb = out_norm_a[l][None], out_norm_b[l][None]
        fw = final_norm_w[None]
        final = l == DEPTH - 1

        ya_s, q_s, k_s, v_s, vn_s, bz_s = _inproj_sample(xs[0], common + (coef_s, mixb_s, ga))
        sample_ops = (rows8(q_s), rows8(k_s), rows8(v_s), kt_all, vt_all, sel, new, btab, wtab)
        ya, q_hp, k_hp, v_hp, kt_stack, vt_stack, bz, o_t = _inproj_prompt(
            xp, common + (mix_p, mixb_p, ga), l, kt_stack, vt_stack, sample_ops)
        o_hp = _attn_prompt(q_hp, k_hp, v_hp, bias_prompt)
        xp = _outproj(xp, ya, o_hp, bz, gb, wout, fw, final)

        o_s = o_t[:, :, :nt].reshape(nb, N_PAIRS, 128, nt).transpose(1, 0, 3, 2).reshape(1, N_PAIRS, n_s, 128)
        xs = _outproj(xs, ya_s[None], o_s, bz_s[None], gb, wout, fw, final)
        sk.append(k_s)
        sv.append(v_s)
        sc.append(vn_s)

    heads = (N_HEADS, HEAD_DIM)
    new_k_prompt = jnp.transpose(kt_stack, (0, 1, 4, 2, 3))
    new_v_prompt = jnp.transpose(vt_stack, (0, 1, 4, 2, 3))
    new_k_sample = jnp.stack(sk).reshape((DEPTH, nb, nt) + heads)
    new_v_sample = jnp.stack(sv).reshape((DEPTH, nb, nt) + heads)
    new_vchunk = jnp.stack(sc).reshape(DEPTH, nb, nt, WIDTH)
    return (xp, xs.reshape(nb, nt, D_MODEL), new_k_prompt, new_v_prompt,
            new_k_sample, new_v_sample, new_vchunk)
```

```python
import functools

import numpy as np
import jax
import jax.numpy as jnp
from jax import lax
from jax.experimental import pallas as pl
from jax.experimental.pallas import tpu as pltpu

D_MODEL = 1024
DEPTH = 4
WIDTH = 512
N_GROUPS = 4
N_HEADS = 8
HEAD_DIM = 64
N_PAIRS = 4
CHUNK = 128
SPAN = 128
REL_BUCKETS = 32
REL_MAX_DIST = 2048
ATTN_SCALE = 0.125
LOG2E = 1.4426950408889634
Q_SCALE = ATTN_SCALE * LOG2E
EPS = 1e-6
NEG = -1e30
N_CLASS = 16
BLK = 128
INPROJ_TILE = 256
OUTPROJ_TILE = 512
ATTN_PHASES = 2
VMEM_LIMIT = 56 * 1024 * 1024

F32 = jnp.float32
BF16 = jnp.bfloat16


def _t5_bucket(dist):
    max_exact = REL_BUCKETS // 2
    large = max_exact + (np.log(np.maximum(dist, 1).astype(np.float32) / max_exact)
                         / np.log(REL_MAX_DIST / max_exact) * (REL_BUCKETS - max_exact)).astype(np.int32)
    large = np.minimum(large, REL_BUCKETS - 1)
    return np.where(dist < max_exact, dist, large).astype(np.int32)


def _rms(x, w):
    return x * lax.rsqrt(jnp.mean(x * x, axis=-1, keepdims=True) + EPS) * w


def _silu(z):
    return z / (1.0 + jnp.exp(-z))


def _full_spec(shape):
    return pl.BlockSpec(shape, lambda *_: (0,) * len(shape))


def _bias_table_kernel(idx_ref, rb_ref, out_ref):
    idx = idx_ref[...]
    for h in range(N_HEADS):
        acc = jnp.full(idx.shape, NEG, F32)
        for bucket in range(REL_BUCKETS):
            acc = jnp.where(idx == bucket, rb_ref[bucket, h] * LOG2E, acc)
        out_ref[h] = acc


def _bias_tables(idx, rel_bias):
    n, r, c = idx.shape
    return pl.pallas_call(
        _bias_table_kernel,
        grid=(n,),
        in_specs=[pl.BlockSpec((None, r, c), lambda i: (i, 0, 0)),
                  pl.BlockSpec(memory_space=pltpu.SMEM)],
        out_specs=pl.BlockSpec((None, N_HEADS, r, c), lambda i: (i, 0, 0, 0)),
        out_shape=jax.ShapeDtypeStruct((n, N_HEADS, r, c), F32),
        compiler_params=pltpu.CompilerParams(dimension_semantics=("parallel",)),
        name="bias_tables",
    )(jnp.asarray(idx, jnp.int32), rel_bias.astype(F32))


N_NEW = 4
ROWS = 8
WB = 2048
N_TILES = WB // 128
NPOS = WB + 128
FAR_TILES = 12
NEAR_TILES = 3
N_PAT = 2 + N_NEW
N_S = 9


def _sample_tables():
    r = np.arange(ROWS)[:, None]
    p = np.arange(NPOS)[None, :]
    i = r % N_NEW
    dist = np.where(p < WB, WB + i - p, i - (p - WB))
    ok = (dist >= 0) & ((p < WB) | (p - WB < N_NEW))
    mult = ((dist <= 128).astype(np.int32) + ((dist % 4 == 0) & (dist <= 512))
            + ((dist % 16 == 0) & (dist <= 2048))) * ok
    idx = np.where(mult > 0, _t5_bucket(np.clip(dist, 0, None)), -1)
    return idx[None], jnp.asarray(mult, F32)


def _sample_selectors():
    lane = np.arange(128)
    sel = np.zeros((ROWS, N_PAT * 128), np.float32)
    new = np.zeros((ROWS, 128), np.float32)
    for i in range(N_NEW):
        sel[i, lane[lane % 16 == i]] = 1.0
        sel[i, 128 + lane[lane % 4 == i]] = 1.0
        sel[i, (2 + i) * 128:(3 + i) * 128] = 1.0
        new[i, i] = 1.0
    return jnp.asarray(sel, BF16), jnp.asarray(new, BF16)


def _sample_specs(sample_ops, layer, row_of):
    sel, new, btab, wtab = sample_ops[5:]
    rows = pl.BlockSpec((None, ROWS, WIDTH), lambda *g: (row_of(*g), 0, 0))
    buf = pl.BlockSpec((None, None, N_HEADS, HEAD_DIM, WB), lambda *g: (layer, row_of(*g), 0, 0, 0))
    in_specs = [rows, rows, rows, buf, buf, _full_spec(sel.shape), _full_spec(new.shape),
                _full_spec(btab.shape), _full_spec(wtab.shape)]
    out_spec = pl.BlockSpec((None, WIDTH, ROWS), lambda *g: (row_of(*g), 0, 0))
    return in_specs, out_spec


def _attn_sample_body(q_ref, k_ref, v_ref, kt_ref, vt_ref, sel_ref, new_ref, b_ref, w_ref):
    tn = (((0,), (0,)), ((), ()))
    nt = (((1,), (1,)), ((), ()))
    as_bf = lambda ref: ref[...].astype(BF16)
    pats = lax.dot_general(as_bf(q_ref), sel_ref[...], tn, preferred_element_type=F32)
    ktn = lax.dot_general(as_bf(k_ref), new_ref[...], tn, preferred_element_type=F32)
    vtn = lax.dot_general(as_bf(v_ref), new_ref[...], tn, preferred_element_type=F32)
    row = lax.broadcasted_iota(jnp.int32, (1, ROWS, 128), 1)
    shape3 = (N_HEADS, ROWS, 128)

    def tile3(x2d, j):
        return x2d[:, j * 128:(j + 1) * 128].reshape(N_HEADS, HEAD_DIM, 128)

    pat16, pat4 = tile3(pats, 0), tile3(pats, 1)
    tok = [tile3(pats, 2 + i) for i in range(N_NEW)]
    s_tiles = []
    for j in range(FAR_TILES + NEAR_TILES):
        pat = pat16 if j < FAR_TILES else pat4
        kt = kt_ref[:, :, j * 128:(j + 1) * 128]
        s_tiles.append(jnp.broadcast_to(jnp.sum(kt * pat, axis=1, keepdims=True), shape3))
    for src in (kt_ref[:, :, WB - 128:], tile3(ktn, 0)):
        t = jnp.zeros(shape3, F32)
        for i in range(N_NEW):
            s_i = jnp.sum(src * tok[i], axis=1, keepdims=True)
            t = jnp.where(row % N_NEW == i, s_i, t)
        s_tiles.append(t)
    ntile = len(s_tiles)
    s_tiles = [s_tiles[j] + b_ref[:, :, j * 128:(j + 1) * 128] for j in range(ntile)]
    m = s_tiles[0]
    for t in s_tiles[1:]:
        m = jnp.maximum(m, t)
    m = jnp.max(m, axis=2, keepdims=True)
    p_tiles = [jnp.exp2(s_tiles[j] - m) * w_ref[:, j * 128:(j + 1) * 128][None] for j in range(ntile)]
    l = p_tiles[0]
    for t in p_tiles[1:]:
        l = l + t
    inv = 1.0 / jnp.sum(l, axis=2, keepdims=True)
    p_tiles = [(t * inv).astype(BF16) for t in p_tiles]

    heads = []
    for h in range(N_HEADS):
        p_win = jnp.concatenate([p_tiles[j][h] for j in range(N_TILES)], axis=1)
        o = lax.dot_general(vt_ref[h].astype(BF16), p_win, nt, preferred_element_type=F32)
        vn_h = tile3(vtn, 0)[h].astype(BF16)
        o = o + lax.dot_general(vn_h, p_tiles[N_TILES][h], nt, preferred_element_type=F32)
        heads.append(o)
    return jnp.concatenate(heads, axis=0)


def _inproj_core(x_ref, nw_ref, win_ref, lnw_ref, lnb_ref, ga_ref, mix_fn):
    h = _rms(x_ref[...], nw_ref[...]).astype(BF16)

    def proj(j):
        return jnp.dot(h, win_ref[:, j * WIDTH:(j + 1) * WIDTH], preferred_element_type=F32)

    a_v = proj(1)
    mu = jnp.mean(a_v, axis=-1, keepdims=True)
    xc = a_v - mu
    vn = xc * lax.rsqrt(jnp.mean(xc * xc, axis=-1, keepdims=True) + EPS) * lnw_ref[...] + lnb_ref[...]
    a_out = proj(0) * mix_fn(vn)
    ya = _rms(a_out * _silu(proj(2)), ga_ref[...]).astype(BF16)
    q = proj(3) * Q_SCALE
    return ya, vn, q, proj(4), proj(5), proj(6)


def _mix_chunks(vn, mix_ref, mixb_ref):
    tm = vn.shape[0]
    chunk = mix_ref.shape[1]
    nch = tm // chunk
    vnb = vn.astype(BF16)
    cols = []
    for g in range(N_GROUPS):
        vg = vnb[:, g * 128:(g + 1) * 128]
        if nch > 1:
            vg = jnp.concatenate([vg[ci * chunk:(ci + 1) * chunk] for ci in range(nch)], axis=1)
        r = jnp.dot(mix_ref[g], vg, preferred_element_type=F32) + mixb_ref[g]
        if nch > 1:
            r = jnp.concatenate([r[:, ci * 128:(ci + 1) * 128] for ci in range(nch)], axis=0)
        cols.append(r)
    return jnp.concatenate(cols, axis=1)


def _mix_new_tokens(vn, coef_ref, mixb_ref):
    cols = []
    for g in range(N_GROUPS):
        vg = vn[:, g * 128:(g + 1) * 128]
        acc = coef_ref[g, 0] * vg + mixb_ref[g]
        for d in range(1, N_NEW):
            acc = acc + coef_ref[g, d] * pltpu.roll(vg, d, axis=0)
        cols.append(acc)
    return jnp.concatenate(cols, axis=1)


N_W = 8


def _inproj_prompt_kernel(*refs):
    x_ref, nw_ref, win_ref, lnw_ref, lnb_ref, mix_ref, mixb_ref, ga_ref = refs[:N_W]
    sample_refs = refs[N_W:N_W + N_S]
    ya_ref, q_ref, k_ref, v_ref, kt_ref, vt_ref, bz_ref, os_ref = refs[-8:]
    tm = x_ref.shape[0]
    ya, _, q, k, v, bz = _inproj_core(
        x_ref, nw_ref, win_ref, lnw_ref, lnb_ref, ga_ref,
        functools.partial(_mix_chunks, mix_ref=mix_ref, mixb_ref=mixb_ref))
    ya_ref[...] = ya
    bz_ref[...] = bz.astype(bz_ref.dtype)
    for hp in range(N_PAIRS):
        sl = slice(hp * 128, (hp + 1) * 128)
        q_ref[hp] = q[:, sl]
        k_ref[hp] = k[:, sl]
        v_ref[hp] = v[:, sl]
    kt_ref[...] = k.T.reshape(N_HEADS, HEAD_DIM, tm)
    vt_ref[...] = v.T.reshape(N_HEADS, HEAD_DIM, tm)
    os_ref[...] = _attn_sample_body(*sample_refs)


def _inproj_sample_kernel(x_ref, nw_ref, win_ref, lnw_ref, lnb_ref, coef_ref, mixb_ref, ga_ref,
                          ya_ref, q_ref, k_ref, v_ref, vn_ref, bz_ref):
    ya, vn, q, k, v, bz = _inproj_core(
        x_ref, nw_ref, win_ref, lnw_ref, lnb_ref, ga_ref,
        functools.partial(_mix_new_tokens, coef_ref=coef_ref, mixb_ref=mixb_ref))
    ya_ref[...] = ya
    q_ref[...] = q
    k_ref[...] = k
    v_ref[...] = v
    vn_ref[...] = vn
    bz_ref[...] = bz.astype(bz_ref.dtype)


def _inproj_prompt(x, weights, layer, kt_stack, vt_stack, sample_ops):
    b, s, _ = x.shape
    tm = INPROJ_TILE
    nj = s // tm
    n_rows = b * nj
    tok = lambda w: pl.BlockSpec((None, tm, w), lambda i, j: (i, j, 0))
    hp_spec = pl.BlockSpec((None, N_PAIRS, tm, 128), lambda i, j: (i, 0, j, 0))
    t_spec = pl.BlockSpec((None, None, N_HEADS, HEAD_DIM, tm), lambda i, j: (layer, i, 0, 0, j))
    hp_shape = jax.ShapeDtypeStruct((b, N_PAIRS, s, 128), F32)
    t_shape = jax.ShapeDtypeStruct((DEPTH, b, N_HEADS, HEAD_DIM, s), F32)
    s_in, s_out = _sample_specs(sample_ops, layer, lambda i, j: i * nj + j)
    in_specs = ([tok(D_MODEL), _full_spec((1, D_MODEL)), _full_spec((D_MODEL, 7 * WIDTH)),
                 _full_spec((1, WIDTH)), _full_spec((1, WIDTH)),
                 _full_spec((N_GROUPS, CHUNK, CHUNK)), _full_spec((N_GROUPS, CHUNK, 1)),
                 _full_spec((1, WIDTH))] + s_in)
    args = (x,) + tuple(weights) + tuple(sample_ops)
    assert len(args) == N_W + N_S
    aliases = {}
    if kt_stack is not None:
        in_specs = in_specs + [pl.BlockSpec(memory_space=pl.ANY)] * 2
        aliases = {len(args): 4, len(args) + 1: 5}
        args = args + (kt_stack, vt_stack)
    return pl.pallas_call(
        _inproj_prompt_kernel,
        grid=(b, nj),
        in_specs=in_specs,
        out_specs=[tok(WIDTH), hp_spec, hp_spec, hp_spec, t_spec, t_spec, tok(WIDTH), s_out],
        out_shape=[jax.ShapeDtypeStruct((b, s, WIDTH), BF16), hp_shape, hp_shape, hp_shape,
                   t_shape, t_shape, jax.ShapeDtypeStruct((b, s, WIDTH), BF16),
                   jax.ShapeDtypeStruct((n_rows, WIDTH, ROWS), F32)],
        input_output_aliases=aliases,
        compiler_params=pltpu.CompilerParams(
            dimension_semantics=("parallel", "parallel"), vmem_limit_bytes=VMEM_LIMIT),
        name="inproj_prompt",
    )(*args)


def _inproj_sample(x, weights):
    n = x.shape[0]
    full = lambda w: pl.BlockSpec((n, w), lambda i: (0, 0))
    sd = lambda dt: jax.ShapeDtypeStruct((n, WIDTH), dt)
    return pl.pallas_call(
        _inproj_sample_kernel,
        grid=(1,),
        in_specs=[full(D_MODEL), _full_spec((1, D_MODEL)), _full_spec((D_MODEL, 7 * WIDTH)),
                  _full_spec((1, WIDTH)), _full_spec((1, WIDTH)),
                  _full_spec((N_GROUPS, N_NEW, n, 1)), _full_spec((N_GROUPS, n, 1)),
                  _full_spec((1, WIDTH))],
        out_specs=[full(WIDTH)] * 6,
        out_shape=[sd(BF16), sd(F32), sd(F32), sd(F32), sd(F32), sd(BF16)],
        compiler_params=pltpu.CompilerParams(
            dimension_semantics=("arbitrary",), vmem_limit_bytes=VMEM_LIMIT),
        name="inproj_sample",
    )(x, *weights)


def _outproj_kernel(x_ref, ya_ref, o_ref, bz_ref, gb_ref, wout_ref, fw_ref, y_ref, *, final):
    o = jnp.concatenate([o_ref[hp] for hp in range(N_PAIRS)], axis=1).astype(F32)
    yb = _rms(o * _silu(bz_ref[...].astype(F32)), gb_ref[...]).astype(BF16)
    ycat = jnp.concatenate([ya_ref[...], yb], axis=1)
    y = x_ref[...] + jnp.dot(ycat, wout_ref[...], preferred_element_type=F32)
    if final:
        y = _rms(y, fw_ref[...])
    y_ref[...] = y


def _outproj(x, ya, o_hp, bz, gb, wout, fw, final):
    b, s, _ = x.shape
    tm = min(OUTPROJ_TILE, s)
    tok = lambda w: pl.BlockSpec((None, tm, w), lambda i, j: (i, j, 0))
    return pl.pallas_call(
        functools.partial(_outproj_kernel, final=final),
        grid=(b, s // tm),
        in_specs=[tok(D_MODEL), tok(WIDTH),
                  pl.BlockSpec((None, N_PAIRS, tm, 128), lambda i, j: (i, 0, j, 0)),
                  tok(WIDTH), _full_spec((1, WIDTH)), _full_spec((2 * WIDTH, D_MODEL)),
                  _full_spec((1, D_MODEL))],
        out_specs=tok(D_MODEL),
        out_shape=jax.ShapeDtypeStruct((b, s, D_MODEL), F32),
        compiler_params=pltpu.CompilerParams(
            dimension_semantics=("parallel", "parallel"), vmem_limit_bytes=VMEM_LIMIT),
        name="outproj",
    )(x, ya, o_hp, bz, gb, wout, fw)


def _prompt_bucket_index():
    qi = np.arange(BLK)[:, None]
    kj = np.arange(2 * BLK)[None, :]
    part, kk = kj // BLK, kj % BLK
    d1 = 16 * (8 * (1 - part) + qi % 8 - kk % 8) + (qi // 8 - kk // 8)
    ok1 = (d1 >= 0) & (d1 <= SPAN)
    j4 = 4 * (32 * (1 - part) + qi % 32 - kk % 32) + (qi // 32 - kk // 32)
    ok4 = (j4 >= 0) & (j4 <= SPAN)
    j16 = qi - kk + 0 * part
    ok16 = (part == 1) & (j16 >= 0)
    dist = np.stack([d1, 4 * j4, 16 * j16])
    ok = np.stack([ok1, ok4, ok16])
    return np.where(ok, _t5_bucket(np.clip(dist, 0, None)), -1)


def _prompt_units(nrow):
    units = [(2, [(c, 0, BLK)], []) for c in range(N_CLASS)]
    for r in range(4):
        cls = [r + 4 * a for a in range(4)]
        for i in range(nrow // 32):
            units.append((1, [(c, 32 * i, 32) for c in cls], [(c, 32 * i - 32, 32) for c in cls] if i else []))
    allc = list(range(N_CLASS))
    for i in range(nrow // 8):
        units.append((0, [(c, 8 * i, 8) for c in allc], [(c, 8 * i - 8, 8) for c in allc] if i else []))
    return units


def _attn_prompt_kernel(*refs):
    q_ref, k_ref, v_ref, bias_ref = refs[:4]
    sample_refs = refs[4:4 + N_S]
    o_ref, os_ref, qs, ks, vs, s16_scr, s_scr, mg_scr, acc_o, acc_l, onat = refs[4 + N_S:]
    nrow = q_ref.shape[0] // N_CLASS
    lane = lax.broadcasted_iota(jnp.int32, (BLK, 128), 1)
    first_head = lane < HEAD_DIM
    units = _prompt_units(nrow)
    phase = pl.program_id(2)

    def gather(ref, lead, pieces):
        return jnp.concatenate([ref[lead + (c, slice(r0, r0 + nr), slice(None))] for c, r0, nr in pieces],
                               axis=0)

    def scatter(ref, lead, pieces, val):
        off = 0
        for c, r0, nr in pieces:
            ref[lead + (c, slice(r0, r0 + nr), slice(None))] = val[off:off + nr]
            off += nr

    def score_slot(u):
        return (s16_scr, u) if u < N_CLASS else (s_scr, u - N_CLASS)

    @pl.when(phase == 0)
    def _scores():
        for c in range(N_CLASS):
            qs[c] = q_ref[pl.ds(c, nrow, stride=N_CLASS), :]
            ks[c] = k_ref[pl.ds(c, nrow, stride=N_CLASS), :]
            vs[c] = v_ref[pl.ds(c, nrow, stride=N_CLASS), :]
        for u, (cfg, q_pieces, prev_pieces) in enumerate(units):
            nk = BLK * (2 if prev_pieces else 1)
            q = gather(qs, (), q_pieces)
            q2 = jnp.concatenate([jnp.where(first_head, q, 0.0), jnp.where(first_head, 0.0, q)], axis=0)
            k = gather(ks, (), prev_pieces + q_pieces).astype(BF16)
            s = lax.dot_general(q2.astype(BF16), k, (((1,), (1,)), ((), ())), preferred_element_type=F32)
            s = s + bias_ref[cfg, :, 2 * BLK - nk:]
            ref, slot = score_slot(u)
            ref[slot, :, :nk] = s
            m = jnp.broadcast_to(jnp.max(s, axis=1, keepdims=True), (2 * BLK, 128))
            for h in range(2):
                mh = m[h * BLK:(h + 1) * BLK]
                if cfg != 2:
                    mh = jnp.maximum(mh, gather(mg_scr, (h,), q_pieces))
                scatter(mg_scr, (h,), q_pieces, mh)

    @pl.when(phase == 1)
    def _softmax():
        for u, (cfg, q_pieces, prev_pieces) in enumerate(units):
            nk = BLK * (2 if prev_pieces else 1)
            v = gather(vs, (), prev_pieces + q_pieces).astype(BF16)
            vext = jnp.concatenate([v, jnp.ones_like(v)], axis=1)
            mq = jnp.concatenate([gather(mg_scr, (0,), q_pieces), gather(mg_scr, (1,), q_pieces)], axis=0)
            if nk > BLK:
                mq = jnp.concatenate([mq, mq], axis=1)
            ref, slot = score_slot(u)
            p = jnp.exp2(ref[slot, :, :nk] - mq).astype(BF16)
            pv = jnp.dot(p, vext, preferred_element_type=F32)
            num = jnp.where(first_head, pv[:BLK, :128], pv[BLK:, :128])
            den = jnp.where(first_head, pv[:BLK, 128:], pv[BLK:, 128:])
            if cfg != 2:
                num = num + gather(acc_o, (), q_pieces)
                den = den + gather(acc_l, (), q_pieces)
            scatter(acc_o, (), q_pieces, num)
            scatter(acc_l, (), q_pieces, den)
        for c in range(N_CLASS):
            onat[pl.ds(c, nrow, stride=N_CLASS), :] = acc_o[c] / acc_l[c]
        o_ref[...] = onat[...].astype(o_ref.dtype)

    os_ref[...] = _attn_sample_body(*sample_refs)


def _attn_prompt(q_hp, k_hp, v_hp, bias_tab, sample_ops, layer, row0):
    b, _, s, _ = q_hp.shape
    nrow = s // N_CLASS
    n_rows = b * N_PAIRS * ATTN_PHASES
    blk = pl.BlockSpec((None, None, s, 128), lambda i, j, r: (i, j, 0, 0))
    cls = lambda lead: pltpu.VMEM(lead + (N_CLASS, nrow, 128), F32)
    n_wide = len(_prompt_units(nrow)) - N_CLASS
    s_in, s_out = _sample_specs(sample_ops, layer,
                                lambda i, j, r: row0 + (i * N_PAIRS + j) * ATTN_PHASES + r)
    s_out = pl.BlockSpec((None, WIDTH, ROWS), lambda i, j, r: ((i * N_PAIRS + j) * ATTN_PHASES + r, 0, 0))
    return pl.pallas_call(
        _attn_prompt_kernel,
        grid=(b, N_PAIRS, ATTN_PHASES),
        in_specs=[blk, blk, blk,
                  pl.BlockSpec((3, None, 2 * BLK, 2 * BLK), lambda i, j, r: (0, j, 0, 0))] + s_in,
        out_specs=[blk, s_out],
        out_shape=[jax.ShapeDtypeStruct(q_hp.shape, BF16),
                   jax.ShapeDtypeStruct((n_rows, WIDTH, ROWS), F32)],
        scratch_shapes=[cls(()), cls(()), cls(()),
                        pltpu.VMEM((N_CLASS, 2 * BLK, BLK), F32),
                        pltpu.VMEM((n_wide, 2 * BLK, 2 * BLK), F32),
                        cls((2,)), cls(()), cls(()),
                        pltpu.VMEM((s, 128), F32)],
        compiler_params=pltpu.CompilerParams(
            dimension_semantics=("parallel", "parallel", "arbitrary"), vmem_limit_bytes=VMEM_LIMIT),
        name="attn_prompt",
    )(q_hp, k_hp, v_hp, bias_tab, *sample_ops)


def _new_token_mix_selectors(nb):
    n = nb * N_NEW
    t = np.arange(n) % N_NEW
    sel_w = np.zeros((N_NEW, N_NEW, N_NEW, n), np.float32)
    sel_b = np.zeros((N_NEW, n), np.float32)
    for r in range(n):
        sel_b[t[r], r] = 1.0
        for d in range(t[r] + 1):
            sel_w[d, t[r], t[r] - d, r] = 1.0
    return jnp.asarray(sel_w), jnp.asarray(sel_b)


def kernel(x_prompt, x_sample, cache_k, cache_v, norm_w, w_in, ln_v_w, ln_v_b, w_spatial, b_spatial,
           rel_bias, out_norm_a, out_norm_b, w_out, final_norm_w):
    b, s, _ = x_prompt.shape
    nb, nt, _ = x_sample.shape
    n_s = nb * nt
    rows_inproj = b * (s // INPROJ_TILE)
    assert cache_k.shape[2] == WB and nt == N_NEW and s % (N_CLASS * BLK) == 0
    assert rows_inproj + b * N_PAIRS * ATTN_PHASES == nb

    kt_all = jnp.transpose(cache_k, (0, 1, 3, 4, 2))
    vt_all = jnp.transpose(cache_v, (0, 1, 3, 4, 2))

    bias_prompt = _bias_tables(_prompt_bucket_index(), rel_bias).reshape(3, N_PAIRS, 2 * BLK, 2 * BLK)
    sample_idx, wtab = _sample_tables()
    btab = _bias_tables(sample_idx, rel_bias)[0]
    sel, new = _sample_selectors()
    sel_w, sel_b = _new_token_mix_selectors(nb)
    hi = lax.Precision.HIGHEST

    tril = np.tril(np.ones((CHUNK, CHUNK), np.float32))
    rows8 = lambda a: jnp.concatenate([a.reshape(nb, nt, WIDTH)] * (ROWS // nt), axis=1)

    xp = x_prompt
    xs = x_sample.reshape(1, n_s, D_MODEL)
    kt_stack = vt_stack = None
    sk, sv, sc = [], [], []
    for l in range(DEPTH):
        win = w_in[l].astype(BF16)
        wout = w_out[l].astype(BF16)
        common = (norm_w[l][None], win, ln_v_w[l][None], ln_v_b[l][None])
        mix_p = (w_spatial[l] * tril).astype(BF16)
        mixb_p = b_spatial[l][:, :, None]
        coef_s = jnp.einsum('gts,dtsr->gdr', w_spatial[l][:, :nt, :nt], sel_w, precision=hi)[..., None]
        mixb_s = jnp.einsum('gt,tr->gr', b_spatial[l][:, :nt], sel_b, precision=hi)[..., None]
        ga, gb = out_norm_a[l][None], out_norm_b[l][None]
        fw = final_norm_w[None]
        final = l == DEPTH - 1

        ya_s, q_s, k_s, v_s, vn_s, bz_s = _inproj_sample(xs[0], common + (coef_s, mixb_s, ga))
        sample_ops = (rows8(q_s), rows8(k_s), rows8(v_s), kt_all, vt_all, sel, new, btab, wtab)
        ya, q_hp, k_hp, v_hp, kt_stack, vt_stack, bz, o_t0 = _inproj_prompt(
            xp, common + (mix_p, mixb_p, ga), l, kt_stack, vt_stack, sample_ops)
        o_hp, o_t1 = _attn_prompt(q_hp, k_hp, v_hp, bias_prompt, sample_ops, l, rows_inproj)
        xp = _outproj(xp, ya, o_hp, bz, gb, wout, fw, final)

        o_t = jnp.concatenate([o_t0, o_t1], axis=0)[:, :, :nt]
        o_s = o_t.reshape(nb, N_PAIRS, 128, nt).transpose(1, 0, 3, 2).reshape(1, N_PAIRS, n_s, 128)
        xs = _outproj(xs, ya_s[None], o_s, bz_s[None], gb, wout, fw, final)
        sk.append(k_s)
        sv.append(v_s)
        sc.append(vn_s)

    heads = (N_HEADS, HEAD_DIM)
    new_k_prompt = jnp.transpose(kt_stack, (0, 1, 4, 2, 3))
    new_v_prompt = jnp.transpose(vt_stack, (0, 1, 4, 2, 3))
    new_k_sample = jnp.stack(sk).reshape((DEPTH, nb, nt) + heads)
    new_v_sample = jnp.stack(sv).reshape((DEPTH, nb, nt) + heads)
    new_vchunk = jnp.stack(sc).reshape(DEPTH, nb, nt, WIDTH)
    return (xp, xs.reshape(nb, nt, D_MODEL), new_k_prompt, new_v_prompt,
            new_k_sample, new_v_sample, new_vchunk)
```

```python
import functools

import numpy as np
import jax
import jax.numpy as jnp
from jax import lax
from jax.experimental import pallas as pl
from jax.experimental.pallas import tpu as pltpu

D_MODEL = 1024
DEPTH = 4
WIDTH = 512
N_GROUPS = 4
N_HEADS = 8
HEAD_DIM = 64
N_PAIRS = 4
CHUNK = 128
SPAN = 128
REL_BUCKETS = 32
REL_MAX_DIST = 2048
ATTN_SCALE = 0.125
LOG2E = 1.4426950408889634
Q_SCALE = ATTN_SCALE * LOG2E
EPS = 1e-6
NEG = -1e30
N_CLASS = 16
BLK = 128
INPROJ_TILE = 256
OUTPROJ_TILE = 1024
ATTN_PHASES = 2
VMEM_LIMIT = 56 * 1024 * 1024

F32 = jnp.float32
BF16 = jnp.bfloat16


def _t5_bucket(dist):
    max_exact = REL_BUCKETS // 2
    large = max_exact + (np.log(np.maximum(dist, 1).astype(np.float32) / max_exact)
                         / np.log(REL_MAX_DIST / max_exact) * (REL_BUCKETS - max_exact)).astype(np.int32)
    large = np.minimum(large, REL_BUCKETS - 1)
    return np.where(dist < max_exact, dist, large).astype(np.int32)


def _rms(x, w):
    return x * lax.rsqrt(jnp.mean(x * x, axis=-1, keepdims=True) + EPS) * w


def _silu(z):
    return z / (1.0 + jnp.exp(-z))


def _full_spec(shape):
    return pl.BlockSpec(shape, lambda *_: (0,) * len(shape))


def _bias_table_kernel(idx_ref, rb_ref, out_ref):
    idx = idx_ref[...]
    for h in range(N_HEADS):
        acc = jnp.full(idx.shape, NEG, F32)
        for bucket in range(REL_BUCKETS):
            acc = jnp.where(idx == bucket, rb_ref[bucket, h] * LOG2E, acc)
        out_ref[h] = acc


def _bias_tables(idx, rel_bias):
    n, r, c = idx.shape
    return pl.pallas_call(
        _bias_table_kernel,
        grid=(n,),
        in_specs=[pl.BlockSpec((None, r, c), lambda i: (i, 0, 0)),
                  pl.BlockSpec(memory_space=pltpu.SMEM)],
        out_specs=pl.BlockSpec((None, N_HEADS, r, c), lambda i: (i, 0, 0, 0)),
        out_shape=jax.ShapeDtypeStruct((n, N_HEADS, r, c), F32),
        compiler_params=pltpu.CompilerParams(dimension_semantics=("parallel",)),
        name="bias_tables",
    )(jnp.asarray(idx, jnp.int32), rel_bias.astype(F32))


N_NEW = 4
ROWS = 8
WB = 2048
N_TILES = WB // 128
NPOS = WB + 128
FAR_TILES = 12
NEAR_TILES = 3
N_PAT = 2 + N_NEW
N_S = 9


def _sample_tables():
    r = np.arange(ROWS)[:, None]
    p = np.arange(NPOS)[None, :]
    i = r % N_NEW
    dist = np.where(p < WB, WB + i - p, i - (p - WB))
    ok = (dist >= 0) & ((p < WB) | (p - WB < N_NEW))
    mult = ((dist <= 128).astype(np.int32) + ((dist % 4 == 0) & (dist <= 512))
            + ((dist % 16 == 0) & (dist <= 2048))) * ok
    idx = np.where(mult > 0, _t5_bucket(np.clip(dist, 0, None)), -1)
    return idx[None], jnp.asarray(mult, F32)


def _sample_selectors():
    lane = np.arange(128)
    sel = np.zeros((ROWS, N_PAT * 128), np.float32)
    new = np.zeros((ROWS, 128), np.float32)
    for i in range(N_NEW):
        sel[i, lane[lane % 16 == i]] = 1.0
        sel[i, 128 + lane[lane % 4 == i]] = 1.0
        sel[i, (2 + i) * 128:(3 + i) * 128] = 1.0
        new[i, i] = 1.0
    return jnp.asarray(sel, BF16), jnp.asarray(new, BF16)


def _sample_specs(sample_ops, layer, row_of):
    sel, new, btab, wtab = sample_ops[5:]
    rows = pl.BlockSpec((None, ROWS, WIDTH), lambda *g: (row_of(*g), 0, 0))
    buf = pl.BlockSpec((None, None, N_HEADS, HEAD_DIM, WB), lambda *g: (layer, row_of(*g), 0, 0, 0))
    in_specs = [rows, rows, rows, buf, buf, _full_spec(sel.shape), _full_spec(new.shape),
                _full_spec(btab.shape), _full_spec(wtab.shape)]
    out_spec = pl.BlockSpec((None, ROWS, WIDTH), lambda *g: (row_of(*g), 0, 0))
    return in_specs, out_spec


def _attn_sample_body(q_ref, k_ref, v_ref, kt_ref, vt_ref, sel_ref, new_ref, b_ref, w_ref, *, mxu_tiles):
    tn = (((0,), (0,)), ((), ()))
    nt = (((1,), (1,)), ((), ()))
    row = lax.broadcasted_iota(jnp.int32, (1, ROWS, 128), 1)
    lane = lax.broadcasted_iota(jnp.int32, (1, 128), 1)
    shape3 = (N_HEADS, ROWS, 128)

    def selected(ref, sel, j):
        t = lax.dot_general(ref[...].astype(BF16), sel, tn, preferred_element_type=F32)
        return [t[:, i * 128:(i + 1) * 128].reshape(N_HEADS, HEAD_DIM, 128) for i in range(j)]

    def token_columns(ref):
        xt = ref[...].T
        return [jnp.broadcast_to(xt[:, i:i + 1], (WIDTH, 128)) for i in range(N_NEW)]

    def by_lane(cols, key):
        out = jnp.zeros((WIDTH, 128), F32)
        for i in range(N_NEW):
            out = jnp.where(key == i, cols[i], out)
        return out.reshape(N_HEADS, HEAD_DIM, 128)

    if mxu_tiles:
        pats = selected(q_ref, sel_ref[...], N_PAT)
        pat16, pat4, tok = pats[0], pats[1], pats[2:]
        (ktn,), (vtn,) = selected(k_ref, new_ref[...], 1), selected(v_ref, new_ref[...], 1)
    else:
        q_cols = token_columns(q_ref)
        tok = [c.reshape(N_HEADS, HEAD_DIM, 128) for c in q_cols]
        pat16, pat4 = by_lane(q_cols, lane % 16), by_lane(q_cols, lane % 4)
        ktn, vtn = by_lane(token_columns(k_ref), lane), by_lane(token_columns(v_ref), lane)
    s_tiles = []
    for j in range(FAR_TILES + NEAR_TILES):
        pat = pat16 if j < FAR_TILES else pat4
        kt = kt_ref[:, :, j * 128:(j + 1) * 128]
        s_tiles.append(jnp.broadcast_to(jnp.sum(kt * pat, axis=1, keepdims=True), shape3))
    for src in (kt_ref[:, :, WB - 128:], ktn):
        t = jnp.zeros(shape3, F32)
        for i in range(N_NEW):
            s_i = jnp.sum(src * tok[i], axis=1, keepdims=True)
            t = jnp.where(row % N_NEW == i, s_i, t)
        s_tiles.append(t)
    ntile = len(s_tiles)
    s_tiles = [s_tiles[j] + b_ref[:, :, j * 128:(j + 1) * 128] for j in range(ntile)]
    m = s_tiles[0]
    for t in s_tiles[1:]:
        m = jnp.maximum(m, t)
    m = jnp.max(m, axis=2, keepdims=True)
    p_tiles = [jnp.exp2(s_tiles[j] - m) * w_ref[:, j * 128:(j + 1) * 128][None] for j in range(ntile)]
    l = p_tiles[0]
    for t in p_tiles[1:]:
        l = l + t
    inv = 1.0 / jnp.sum(l, axis=2, keepdims=True)
    p_tiles = [(t * inv).astype(BF16) for t in p_tiles]

    heads = []
    for h in range(N_HEADS):
        p_win = jnp.concatenate([p_tiles[j][h] for j in range(N_TILES)], axis=1)
        o = lax.dot_general(vt_ref[h].astype(BF16), p_win, nt, preferred_element_type=F32)
        vn_h = vtn[h].astype(BF16)
        o = o + lax.dot_general(vn_h, p_tiles[N_TILES][h], nt, preferred_element_type=F32)
        heads.append(o)
    return jnp.concatenate(heads, axis=0).T


def _inproj_core(x_ref, nw_ref, win_ref, lnw_ref, lnb_ref, ga_ref, mix_fn):
    h = _rms(x_ref[...], nw_ref[...]).astype(BF16)

    def proj(j):
        return jnp.dot(h, win_ref[:, j * WIDTH:(j + 1) * WIDTH], preferred_element_type=F32)

    a_v = proj(1)
    mu = jnp.mean(a_v, axis=-1, keepdims=True)
    xc = a_v - mu
    vn = xc * lax.rsqrt(jnp.mean(xc * xc, axis=-1, keepdims=True) + EPS) * lnw_ref[...] + lnb_ref[...]
    a_out = proj(0) * mix_fn(vn)
    ya = _rms(a_out * _silu(proj(2)), ga_ref[...]).astype(BF16)
    q = proj(3) * Q_SCALE
    return ya, vn, q, proj(4), proj(5), proj(6)


def _mix_chunks(vn, mix_ref, mixb_ref):
    tm = vn.shape[0]
    chunk = mix_ref.shape[1]
    nch = tm // chunk
    vnb = vn.astype(BF16)
    cols = []
    for g in range(N_GROUPS):
        vg = vnb[:, g * 128:(g + 1) * 128]
        if nch > 1:
            vg = jnp.concatenate([vg[ci * chunk:(ci + 1) * chunk] for ci in range(nch)], axis=1)
        r = jnp.dot(mix_ref[g], vg, preferred_element_type=F32) + mixb_ref[g]
        if nch > 1:
            r = jnp.concatenate([r[:, ci * 128:(ci + 1) * 128] for ci in range(nch)], axis=0)
        cols.append(r)
    return jnp.concatenate(cols, axis=1)


def _mix_new_tokens(vn, coef_ref, mixb_ref):
    cols = []
    for g in range(N_GROUPS):
        vg = vn[:, g * 128:(g + 1) * 128]
        acc = coef_ref[g, 0] * vg + mixb_ref[g]
        for d in range(1, N_NEW):
            acc = acc + coef_ref[g, d] * pltpu.roll(vg, d, axis=0)
        cols.append(acc)
    return jnp.concatenate(cols, axis=1)


N_W = 8


def _inproj_prompt_kernel(*refs):
    x_ref, nw_ref, win_ref, lnw_ref, lnb_ref, mix_ref, mixb_ref, ga_ref = refs[:N_W]
    sample_refs = refs[N_W:N_W + N_S]
    ya_ref, q_ref, k_ref, v_ref, kt_ref, vt_ref, bz_ref, os_ref = refs[-8:]
    tm = x_ref.shape[0]
    ya, _, q, k, v, bz = _inproj_core(
        x_ref, nw_ref, win_ref, lnw_ref, lnb_ref, ga_ref,
        functools.partial(_mix_chunks, mix_ref=mix_ref, mixb_ref=mixb_ref))
    ya_ref[...] = ya
    bz_ref[...] = bz.astype(bz_ref.dtype)
    for hp in range(N_PAIRS):
        sl = slice(hp * 128, (hp + 1) * 128)
        q_ref[hp] = q[:, sl]
        k_ref[hp] = k[:, sl]
        v_ref[hp] = v[:, sl]
    kt_ref[...] = k.T.reshape(N_HEADS, HEAD_DIM, tm)
    vt_ref[...] = v.T.reshape(N_HEADS, HEAD_DIM, tm)
    os_ref[...] = _attn_sample_body(*sample_refs, mxu_tiles=False)


def _inproj_sample_kernel(x_ref, nw_ref, win_ref, lnw_ref, lnb_ref, coef_ref, mixb_ref, ga_ref,
                          ya_ref, q_ref, k_ref, v_ref, vn_ref, bz_ref):
    ya, vn, q, k, v, bz = _inproj_core(
        x_ref, nw_ref, win_ref, lnw_ref, lnb_ref, ga_ref,
        functools.partial(_mix_new_tokens, coef_ref=coef_ref, mixb_ref=mixb_ref))
    ya_ref[...] = ya
    q_ref[...] = q
    k_ref[...] = k
    v_ref[...] = v
    vn_ref[...] = vn
    bz_ref[...] = bz.astype(bz_ref.dtype)


def _inproj_prompt(x, weights, layer, kt_stack, vt_stack, sample_ops):
    b, s, _ = x.shape
    tm = INPROJ_TILE
    nj = s // tm
    n_rows = b * nj
    tok = lambda w: pl.BlockSpec((None, tm, w), lambda i, j: (i, j, 0))
    hp_spec = pl.BlockSpec((None, N_PAIRS, tm, 128), lambda i, j: (i, 0, j, 0))
    t_spec = pl.BlockSpec((None, None, N_HEADS, HEAD_DIM, tm), lambda i, j: (layer, i, 0, 0, j))
    hp_shape = jax.ShapeDtypeStruct((b, N_PAIRS, s, 128), F32)
    t_shape = jax.ShapeDtypeStruct((DEPTH, b, N_HEADS, HEAD_DIM, s), F32)
    s_in, s_out = _sample_specs(sample_ops, layer, lambda i, j: i * nj + j)
    in_specs = ([tok(D_MODEL), _full_spec((1, D_MODEL)), _full_spec((D_MODEL, 7 * WIDTH)),
                 _full_spec((1, WIDTH)), _full_spec((1, WIDTH)),
                 _full_spec((N_GROUPS, CHUNK, CHUNK)), _full_spec((N_GROUPS, CHUNK, 1)),
                 _full_spec((1, WIDTH))] + s_in)
    args = (x,) + tuple(weights) + tuple(sample_ops)
    assert len(args) == N_W + N_S
    aliases = {}
    if kt_stack is not None:
        in_specs = in_specs + [pl.BlockSpec(memory_space=pl.ANY)] * 2
        aliases = {len(args): 4, len(args) + 1: 5}
        args = args + (kt_stack, vt_stack)
    return pl.pallas_call(
        _inproj_prompt_kernel,
        grid=(b, nj),
        in_specs=in_specs,
        out_specs=[tok(WIDTH), hp_spec, hp_spec, hp_spec, t_spec, t_spec, tok(WIDTH), s_out],
        out_shape=[jax.ShapeDtypeStruct((b, s, WIDTH), BF16), hp_shape, hp_shape, hp_shape,
                   t_shape, t_shape, jax.ShapeDtypeStruct((b, s, WIDTH), BF16),
                   jax.ShapeDtypeStruct((n_rows, ROWS, WIDTH), F32)],
        input_output_aliases=aliases,
        compiler_params=pltpu.CompilerParams(
            dimension_semantics=("parallel", "parallel"), vmem_limit_bytes=VMEM_LIMIT),
        name="inproj_prompt",
    )(*args)


def _inproj_sample(x, weights):
    n = x.shape[0]
    full = lambda w: pl.BlockSpec((n, w), lambda i: (0, 0))
    sd = lambda dt: jax.ShapeDtypeStruct((n, WIDTH), dt)
    return pl.pallas_call(
        _inproj_sample_kernel,
        grid=(1,),
        in_specs=[full(D_MODEL), _full_spec((1, D_MODEL)), _full_spec((D_MODEL, 7 * WIDTH)),
                  _full_spec((1, WIDTH)), _full_spec((1, WIDTH)),
                  _full_spec((N_GROUPS, N_NEW, n, 1)), _full_spec((N_GROUPS, n, 1)),
                  _full_spec((1, WIDTH))],
        out_specs=[full(WIDTH)] * 6,
        out_shape=[sd(BF16), sd(F32), sd(F32), sd(F32), sd(F32), sd(BF16)],
        compiler_params=pltpu.CompilerParams(
            dimension_semantics=("arbitrary",), vmem_limit_bytes=VMEM_LIMIT),
        name="inproj_sample",
    )(x, *weights)


def _outproj_kernel(x_ref, ya_ref, o_ref, bz_ref, gb_ref, wout_ref, fw_ref, y_ref, *, final):
    if len(o_ref.shape) == 3:
        o = jnp.concatenate([o_ref[hp] for hp in range(N_PAIRS)], axis=1).astype(F32)
    else:
        o = o_ref[...].astype(F32)
    yb = _rms(o * _silu(bz_ref[...].astype(F32)), gb_ref[...]).astype(BF16)
    ycat = jnp.concatenate([ya_ref[...], yb], axis=1)
    y = x_ref[...] + jnp.dot(ycat, wout_ref[...], preferred_element_type=F32)
    if final:
        y = _rms(y, fw_ref[...])
    y_ref[...] = y


def _outproj(x, ya, o, bz, gb, wout, fw, final):
    b, s, _ = x.shape
    tm = min(OUTPROJ_TILE, s)
    tok = lambda w: pl.BlockSpec((None, tm, w), lambda i, j: (i, j, 0))
    o_spec = tok(WIDTH) if o.ndim == 3 else pl.BlockSpec((None, N_PAIRS, tm, 128), lambda i, j: (i, 0, j, 0))
    return pl.pallas_call(
        functools.partial(_outproj_kernel, final=final),
        grid=(b, s // tm),
        in_specs=[tok(D_MODEL), tok(WIDTH), o_spec,
                  tok(WIDTH), _full_spec((1, WIDTH)), _full_spec((2 * WIDTH, D_MODEL)),
                  _full_spec((1, D_MODEL))],
        out_specs=tok(D_MODEL),
        out_shape=jax.ShapeDtypeStruct((b, s, D_MODEL), F32),
        compiler_params=pltpu.CompilerParams(
            dimension_semantics=("parallel", "parallel"), vmem_limit_bytes=VMEM_LIMIT),
        name="outproj",
    )(x, ya, o, bz, gb, wout, fw)


def _prompt_bucket_index():
    qi = np.arange(BLK)[:, None]
    kj = np.arange(2 * BLK)[None, :]
    part, kk = kj // BLK, kj % BLK
    d1 = 16 * (8 * (1 - part) + qi % 8 - kk % 8) + (qi // 8 - kk // 8)
    ok1 = (d1 >= 0) & (d1 <= SPAN)
    j4 = 4 * (32 * (1 - part) + qi % 32 - kk % 32) + (qi // 32 - kk // 32)
    ok4 = (j4 >= 0) & (j4 <= SPAN)
    j16 = qi - kk + 0 * part
    ok16 = (part == 1) & (j16 >= 0)
    dist = np.stack([d1, 4 * j4, 16 * j16])
    ok = np.stack([ok1, ok4, ok16])
    return np.where(ok, _t5_bucket(np.clip(dist, 0, None)), -1)


def _prompt_units(nrow):
    units = [(2, [(c, 0, BLK)], []) for c in range(N_CLASS)]
    for r in range(4):
        cls = [r + 4 * a for a in range(4)]
        for i in range(nrow // 32):
            units.append((1, [(c, 32 * i, 32) for c in cls], [(c, 32 * i - 32, 32) for c in cls] if i else []))
    allc = list(range(N_CLASS))
    for i in range(nrow // 8):
        units.append((0, [(c, 8 * i, 8) for c in allc], [(c, 8 * i - 8, 8) for c in allc] if i else []))
    return units


def _attn_prompt_kernel(*refs):
    q_ref, k_ref, v_ref, bias_ref = refs[:4]
    sample_refs = refs[4:4 + N_S]
    o_ref, os_ref, qs, ks, vs, s16_scr, s_scr, mg_scr, acc_o, acc_l, onat = refs[4 + N_S:]
    nrow = q_ref.shape[0] // N_CLASS
    lane = lax.broadcasted_iota(jnp.int32, (BLK, 128), 1)
    first_head = lane < HEAD_DIM
    units = _prompt_units(nrow)
    phase = pl.program_id(2)

    def gather(ref, lead, pieces):
        return jnp.concatenate([ref[lead + (c, slice(r0, r0 + nr), slice(None))] for c, r0, nr in pieces],
                               axis=0)

    def scatter(ref, lead, pieces, val):
        off = 0
        for c, r0, nr in pieces:
            ref[lead + (c, slice(r0, r0 + nr), slice(None))] = val[off:off + nr]
            off += nr

    def score_slot(u):
        return (s16_scr, u) if u < N_CLASS else (s_scr, u - N_CLASS)

    @pl.when(phase == 0)
    def _scores():
        for c in range(N_CLASS):
            qs[c] = q_ref[pl.ds(c, nrow, stride=N_CLASS), :]
            ks[c] = k_ref[pl.ds(c, nrow, stride=N_CLASS), :]
            vs[c] = v_ref[pl.ds(c, nrow, stride=N_CLASS), :]
        for u, (cfg, q_pieces, prev_pieces) in enumerate(units):
            nk = BLK * (2 if prev_pieces else 1)
            q = gather(qs, (), q_pieces)
            q2 = jnp.concatenate([jnp.where(first_head, q, 0.0), jnp.where(first_head, 0.0, q)], axis=0)
            k = gather(ks, (), prev_pieces + q_pieces).astype(BF16)
            s = lax.dot_general(q2.astype(BF16), k, (((1,), (1,)), ((), ())), preferred_element_type=F32)
            s = s + bias_ref[cfg, :, 2 * BLK - nk:]
            ref, slot = score_slot(u)
            ref[slot, :, :nk] = s
            m = jnp.broadcast_to(jnp.max(s, axis=1, keepdims=True), (2 * BLK, 128))
            for h in range(2):
                mh = m[h * BLK:(h + 1) * BLK]
                if cfg != 2:
                    mh = jnp.maximum(mh, gather(mg_scr, (h,), q_pieces))
                scatter(mg_scr, (h,), q_pieces, mh)

    @pl.when(phase == 1)
    def _softmax():
        for u, (cfg, q_pieces, prev_pieces) in enumerate(units):
            nk = BLK * (2 if prev_pieces else 1)
            v = gather(vs, (), prev_pieces + q_pieces).astype(BF16)
            vext = jnp.concatenate([v, jnp.ones_like(v)], axis=1)
            mq = jnp.concatenate([gather(mg_scr, (0,), q_pieces), gather(mg_scr, (1,), q_pieces)], axis=0)
            if nk > BLK:
                mq = jnp.concatenate([mq, mq], axis=1)
            ref, slot = score_slot(u)
            p = jnp.exp2(ref[slot, :, :nk] - mq).astype(BF16)
            pv = jnp.dot(p, vext, preferred_element_type=F32)
            num = jnp.where(first_head, pv[:BLK, :128], pv[BLK:, :128])
            den = jnp.where(first_head, pv[:BLK, 128:], pv[BLK:, 128:])
            if cfg != 2:
                num = num + gather(acc_o, (), q_pieces)
                den = den + gather(acc_l, (), q_pieces)
            scatter(acc_o, (), q_pieces, num)
            scatter(acc_l, (), q_pieces, den)
        for c in range(N_CLASS):
            onat[pl.ds(c, nrow, stride=N_CLASS), :] = acc_o[c] / acc_l[c]
        o_ref[...] = onat[...].astype(o_ref.dtype)

    os_ref[...] = _attn_sample_body(*sample_refs, mxu_tiles=True)


def _attn_prompt(q_hp, k_hp, v_hp, bias_tab, sample_ops, layer, row0):
    b, _, s, _ = q_hp.shape
    nrow = s // N_CLASS
    n_rows = b * N_PAIRS * ATTN_PHASES
    blk = pl.BlockSpec((None, None, s, 128), lambda i, j, r: (i, j, 0, 0))
    cls = lambda lead: pltpu.VMEM(lead + (N_CLASS, nrow, 128), F32)
    n_wide = len(_prompt_units(nrow)) - N_CLASS
    s_in, s_out = _sample_specs(sample_ops, layer,
                                lambda i, j, r: row0 + (i * N_PAIRS + j) * ATTN_PHASES + r)
    s_out = pl.BlockSpec((None, ROWS, WIDTH), lambda i, j, r: ((i * N_PAIRS + j) * ATTN_PHASES + r, 0, 0))
    return pl.pallas_call(
        _attn_prompt_kernel,
        grid=(b, N_PAIRS, ATTN_PHASES),
        in_specs=[blk, blk, blk,
                  pl.BlockSpec((3, None, 2 * BLK, 2 * BLK), lambda i, j, r: (0, j, 0, 0))] + s_in,
        out_specs=[blk, s_out],
        out_shape=[jax.ShapeDtypeStruct(q_hp.shape, BF16),
                   jax.ShapeDtypeStruct((n_rows, ROWS, WIDTH), F32)],
        scratch_shapes=[cls(()), cls(()), cls(()),
                        pltpu.VMEM((N_CLASS, 2 * BLK, BLK), F32),
                        pltpu.VMEM((n_wide, 2 * BLK, 2 * BLK), F32),
                        cls((2,)), cls(()), cls(()),
                        pltpu.VMEM((s, 128), F32)],
        compiler_params=pltpu.CompilerParams(
            dimension_semantics=("parallel", "parallel", "arbitrary"), vmem_limit_bytes=VMEM_LIMIT),
        name="attn_prompt",
    )(q_hp, k_hp, v_hp, bias_tab, *sample_ops)


def _new_token_mix_selectors(nb):
    n = nb * N_NEW
    t = np.arange(n) % N_NEW
    sel_w = np.zeros((N_NEW, N_NEW, N_NEW, n), np.float32)
    sel_b = np.zeros((N_NEW, n), np.float32)
    for r in range(n):
        sel_b[t[r], r] = 1.0
        for d in range(t[r] + 1):
            sel_w[d, t[r], t[r] - d, r] = 1.0
    return jnp.asarray(sel_w), jnp.asarray(sel_b)


def kernel(x_prompt, x_sample, cache_k, cache_v, norm_w, w_in, ln_v_w, ln_v_b, w_spatial, b_spatial,
           rel_bias, out_norm_a, out_norm_b, w_out, final_norm_w):
    b, s, _ = x_prompt.shape
    nb, nt, _ = x_sample.shape
    n_s = nb * nt
    rows_inproj = b * (s // INPROJ_TILE)
    assert cache_k.shape[2] == WB and nt == N_NEW and s % (N_CLASS * BLK) == 0
    assert rows_inproj + b * N_PAIRS * ATTN_PHASES == nb

    kt_all = jnp.transpose(cache_k, (0, 1, 3, 4, 2))
    vt_all = jnp.transpose(cache_v, (0, 1, 3, 4, 2))

    bias_prompt = _bias_tables(_prompt_bucket_index(), rel_bias).reshape(3, N_PAIRS, 2 * BLK, 2 * BLK)
    sample_idx, wtab = _sample_tables()
    btab = _bias_tables(sample_idx, rel_bias)[0]
    sel, new = _sample_selectors()
    sel_w, sel_b = _new_token_mix_selectors(nb)
    hi = lax.Precision.HIGHEST

    tril = np.tril(np.ones((CHUNK, CHUNK), np.float32))
    rows8 = lambda a: jnp.concatenate([a.reshape(nb, nt, WIDTH)] * (ROWS // nt), axis=1)

    xp = x_prompt
    xs = x_sample.reshape(1, n_s, D_MODEL)
    kt_stack = vt_stack = None
    sk, sv, sc = [], [], []
    for l in range(DEPTH):
        win = w_in[l].astype(BF16)
        wout = w_out[l].astype(BF16)
        common = (norm_w[l][None], win, ln_v_w[l][None], ln_v_b[l][None])
        mix_p = (w_spatial[l] * tril).astype(BF16)
        mixb_p = b_spatial[l][:, :, None]
        coef_s = jnp.einsum('gts,dtsr->gdr', w_spatial[l][:, :nt, :nt], sel_w, precision=hi)[..., None]
        mixb_s = jnp.einsum('gt,tr->gr', b_spatial[l][:, :nt], sel_b, precision=hi)[..., None]
        ga, gb = out_norm_a[l][None], out_norm_b[l][None]
        fw = final_norm_w[None]
        final = l == DEPTH - 1

        ya_s, q_s, k_s, v_s, vn_s, bz_s = _inproj_sample(xs[0], common + (coef_s, mixb_s, ga))
        sample_ops = (rows8(q_s), rows8(k_s), rows8(v_s), kt_all, vt_all, sel, new, btab, wtab)
        ya, q_hp, k_hp, v_hp, kt_stack, vt_stack, bz, o_t0 = _inproj_prompt(
            xp, common + (mix_p, mixb_p, ga), l, kt_stack, vt_stack, sample_ops)
        o_hp, o_t1 = _attn_prompt(q_hp, k_hp, v_hp, bias_prompt, sample_ops, l, rows_inproj)
        xp = _outproj(xp, ya, o_hp, bz, gb, wout, fw, final)

        o_s = jnp.concatenate([o_t0, o_t1], axis=0)[:, :nt, :].reshape(1, n_s, WIDTH)
        xs = _outproj(xs, ya_s[None], o_s, bz_s[None], gb, wout, fw, final)
        sk.append(k_s)
        sv.append(v_s)
        sc.append(vn_s)

    heads = (N_HEADS, HEAD_DIM)
    new_k_prompt = jnp.transpose(kt_stack, (0, 1, 4, 2, 3))
    new_v_prompt = jnp.transpose(vt_stack, (0, 1, 4, 2, 3))
    new_k_sample = jnp.stack(sk).reshape((DEPTH, nb, nt) + heads)
    new_v_sample = jnp.stack(sv).reshape((DEPTH, nb, nt) + heads)
    new_vchunk = jnp.stack(sc).reshape(DEPTH, nb, nt, WIDTH)
    return (xp, xs.reshape(nb, nt, D_MODEL), new_k_prompt, new_v_prompt,
            new_k_sample, new_v_sample, new_vchunk)
```

```python
import functools

import numpy as np
import jax
import jax.numpy as jnp
from jax import lax
from jax.experimental import pallas as pl
from jax.experimental.pallas import tpu as pltpu

D_MODEL = 1024
DEPTH = 4
WIDTH = 512
N_GROUPS = 4
N_HEADS = 8
HEAD_DIM = 64
N_PAIRS = 4
CHUNK = 128
SPAN = 128
REL_BUCKETS = 32
REL_MAX_DIST = 2048
ATTN_SCALE = 0.125
LOG2E = 1.4426950408889634
Q_SCALE = ATTN_SCALE * LOG2E
EPS = 1e-6
NEG = -1e30
N_CLASS = 16
BLK = 128
INPROJ_TILE = 256
OUTPROJ_TILE = 1024
ATTN_PHASES = 2
VMEM_LIMIT = 56 * 1024 * 1024

F32 = jnp.float32
BF16 = jnp.bfloat16


def _t5_bucket(dist):
    max_exact = REL_BUCKETS // 2
    large = max_exact + (np.log(np.maximum(dist, 1).astype(np.float32) / max_exact)
                         / np.log(REL_MAX_DIST / max_exact) * (REL_BUCKETS - max_exact)).astype(np.int32)
    large = np.minimum(large, REL_BUCKETS - 1)
    return np.where(dist < max_exact, dist, large).astype(np.int32)


def _rms(x, w):
    return x * lax.rsqrt(jnp.mean(x * x, axis=-1, keepdims=True) + EPS) * w


def _silu(z):
    return z / (1.0 + jnp.exp(-z))


def _full_spec(shape):
    return pl.BlockSpec(shape, lambda *_: (0,) * len(shape))


def _bias_table_kernel(idx_ref, rb_ref, out_ref):
    idx = idx_ref[...]
    for h in range(N_HEADS):
        acc = jnp.full(idx.shape, NEG, F32)
        for bucket in range(REL_BUCKETS):
            acc = jnp.where(idx == bucket, rb_ref[bucket, h] * LOG2E, acc)
        out_ref[h] = acc


def _bias_tables(idx, rel_bias):
    n, r, c = idx.shape
    return pl.pallas_call(
        _bias_table_kernel,
        grid=(n,),
        in_specs=[pl.BlockSpec((None, r, c), lambda i: (i, 0, 0)),
                  pl.BlockSpec(memory_space=pltpu.SMEM)],
        out_specs=pl.BlockSpec((None, N_HEADS, r, c), lambda i: (i, 0, 0, 0)),
        out_shape=jax.ShapeDtypeStruct((n, N_HEADS, r, c), F32),
        compiler_params=pltpu.CompilerParams(dimension_semantics=("parallel",)),
        name="bias_tables",
    )(jnp.asarray(idx, jnp.int32), rel_bias.astype(F32))


N_NEW = 4
ROWS = 8
WB = 2048
N_TILES = WB // 128
NPOS = WB + 128
FAR_TILES = 12
NEAR_TILES = 3
N_PAT = 2 + N_NEW
N_S = 9


def _sample_tables():
    r = np.arange(ROWS)[:, None]
    p = np.arange(NPOS)[None, :]
    i = r % N_NEW
    dist = np.where(p < WB, WB + i - p, i - (p - WB))
    ok = (dist >= 0) & ((p < WB) | (p - WB < N_NEW))
    mult = ((dist <= 128).astype(np.int32) + ((dist % 4 == 0) & (dist <= 512))
            + ((dist % 16 == 0) & (dist <= 2048))) * ok
    idx = np.where(mult > 0, _t5_bucket(np.clip(dist, 0, None)), -1)
    return idx[None], jnp.asarray(mult, F32)


def _sample_selectors():
    lane = np.arange(128)
    sel = np.zeros((ROWS, N_PAT * 128), np.float32)
    new = np.zeros((ROWS, 128), np.float32)
    for i in range(N_NEW):
        sel[i, lane[lane % 16 == i]] = 1.0
        sel[i, 128 + lane[lane % 4 == i]] = 1.0
        sel[i, (2 + i) * 128:(3 + i) * 128] = 1.0
        new[i, i] = 1.0
    return jnp.asarray(sel, BF16), jnp.asarray(new, BF16)


def _sample_specs(sample_ops, layer, row_of):
    sel, new, btab, wtab = sample_ops[5:]
    rows = pl.BlockSpec((None, ROWS, WIDTH), lambda *g: (row_of(*g), 0, 0))
    buf = pl.BlockSpec((None, None, N_HEADS, HEAD_DIM, WB), lambda *g: (layer, row_of(*g), 0, 0, 0))
    in_specs = [rows, rows, rows, buf, buf, _full_spec(sel.shape), _full_spec(new.shape),
                _full_spec(btab.shape), _full_spec(wtab.shape)]
    out_spec = pl.BlockSpec((None, ROWS, WIDTH), lambda *g: (row_of(*g), 0, 0))
    return in_specs, out_spec


def _attn_sample_body(q_ref, k_ref, v_ref, kt_ref, vt_ref, sel_ref, new_ref, b_ref, w_ref, *, mxu_tiles):
    tn = (((0,), (0,)), ((), ()))
    nt = (((1,), (1,)), ((), ()))
    row = lax.broadcasted_iota(jnp.int32, (1, ROWS, 128), 1)
    lane = lax.broadcasted_iota(jnp.int32, (1, 128), 1)
    shape3 = (N_HEADS, ROWS, 128)

    def selected(ref, sel, j):
        t = lax.dot_general(ref[...].astype(BF16), sel, tn, preferred_element_type=F32)
        return [t[:, i * 128:(i + 1) * 128].reshape(N_HEADS, HEAD_DIM, 128) for i in range(j)]

    def token_columns(ref):
        xt = ref[...].T
        return [jnp.broadcast_to(xt[:, i:i + 1], (WIDTH, 128)) for i in range(N_NEW)]

    def by_lane(cols, key):
        out = jnp.zeros((WIDTH, 128), F32)
        for i in range(N_NEW):
            out = jnp.where(key == i, cols[i], out)
        return out.reshape(N_HEADS, HEAD_DIM, 128)

    if mxu_tiles:
        pats = selected(q_ref, sel_ref[...], N_PAT)
        pat16, pat4, tok = pats[0], pats[1], pats[2:]
        (ktn,), (vtn,) = selected(k_ref, new_ref[...], 1), selected(v_ref, new_ref[...], 1)
    else:
        q_cols = token_columns(q_ref)
        tok = [c.reshape(N_HEADS, HEAD_DIM, 128) for c in q_cols]
        pat16, pat4 = by_lane(q_cols, lane % 16), by_lane(q_cols, lane % 4)
        ktn, vtn = by_lane(token_columns(k_ref), lane), by_lane(token_columns(v_ref), lane)
    s_tiles = []
    for j in range(FAR_TILES + NEAR_TILES):
        pat = pat16 if j < FAR_TILES else pat4
        kt = kt_ref[:, :, j * 128:(j + 1) * 128]
        s_tiles.append(jnp.broadcast_to(jnp.sum(kt * pat, axis=1, keepdims=True), shape3))
    for src in (kt_ref[:, :, WB - 128:], ktn):
        t = jnp.zeros(shape3, F32)
        for i in range(N_NEW):
            s_i = jnp.sum(src * tok[i], axis=1, keepdims=True)
            t = jnp.where(row % N_NEW == i, s_i, t)
        s_tiles.append(t)
    ntile = len(s_tiles)
    s_tiles = [s_tiles[j] + b_ref[:, :, j * 128:(j + 1) * 128] for j in range(ntile)]
    m = s_tiles[0]
    for t in s_tiles[1:]:
        m = jnp.maximum(m, t)
    m = jnp.max(m, axis=2, keepdims=True)
    p_tiles = [jnp.exp2(s_tiles[j] - m) * w_ref[:, j * 128:(j + 1) * 128][None] for j in range(ntile)]
    l = p_tiles[0]
    for t in p_tiles[1:]:
        l = l + t
    inv = 1.0 / jnp.sum(l, axis=2, keepdims=True)
    p_tiles = [(t * inv).astype(BF16) for t in p_tiles]

    heads = []
    for h in range(N_HEADS):
        p_win = jnp.concatenate([p_tiles[j][h] for j in range(N_TILES)], axis=1)
        o = lax.dot_general(vt_ref[h].astype(BF16), p_win, nt, preferred_element_type=F32)
        vn_h = vtn[h].astype(BF16)
        o = o + lax.dot_general(vn_h, p_tiles[N_TILES][h], nt, preferred_element_type=F32)
        heads.append(o)
    return jnp.concatenate(heads, axis=0).T


def _inproj_core(x_ref, nw_ref, win_ref, lnw_ref, lnb_ref, ga_ref, mix_fn):
    h = _rms(x_ref[...], nw_ref[...]).astype(BF16)

    def proj(j):
        return jnp.dot(h, win_ref[:, j * WIDTH:(j + 1) * WIDTH], preferred_element_type=F32)

    a_v = proj(1)
    mu = jnp.mean(a_v, axis=-1, keepdims=True)
    xc = a_v - mu
    vn = xc * lax.rsqrt(jnp.mean(xc * xc, axis=-1, keepdims=True) + EPS) * lnw_ref[...] + lnb_ref[...]
    a_out = proj(0) * mix_fn(vn)
    ya = _rms(a_out * _silu(proj(2)), ga_ref[...]).astype(BF16)
    q = proj(3) * Q_SCALE
    return ya, vn, q, proj(4), proj(5), proj(6)


def _mix_chunks(vn, mix_ref, mixb_ref):
    tm = vn.shape[0]
    chunk = mix_ref.shape[1]
    nch = tm // chunk
    vnb = vn.astype(BF16)
    cols = []
    for g in range(N_GROUPS):
        vg = vnb[:, g * 128:(g + 1) * 128]
        if nch > 1:
            vg = jnp.concatenate([vg[ci * chunk:(ci + 1) * chunk] for ci in range(nch)], axis=1)
        r = jnp.dot(mix_ref[g], vg, preferred_element_type=F32) + mixb_ref[g]
        if nch > 1:
            r = jnp.concatenate([r[:, ci * 128:(ci + 1) * 128] for ci in range(nch)], axis=0)
        cols.append(r)
    return jnp.concatenate(cols, axis=1)


def _mix_new_tokens(vn, coef_ref, mixb_ref):
    cols = []
    for g in range(N_GROUPS):
        vg = vn[:, g * 128:(g + 1) * 128]
        acc = coef_ref[g, 0] * vg + mixb_ref[g]
        for d in range(1, N_NEW):
            acc = acc + coef_ref[g, d] * pltpu.roll(vg, d, axis=0)
        cols.append(acc)
    return jnp.concatenate(cols, axis=1)


N_W = 8


def _inproj_prompt_kernel(*refs):
    x_ref, nw_ref, win_ref, lnw_ref, lnb_ref, mix_ref, mixb_ref, ga_ref = refs[:N_W]
    sample_refs = refs[N_W:N_W + N_S]
    ya_ref, q_ref, k_ref, v_ref, kt_ref, vt_ref, bz_ref, os_ref = refs[-8:]
    tm = x_ref.shape[0]
    ya, _, q, k, v, bz = _inproj_core(
        x_ref, nw_ref, win_ref, lnw_ref, lnb_ref, ga_ref,
        functools.partial(_mix_chunks, mix_ref=mix_ref, mixb_ref=mixb_ref))
    ya_ref[...] = ya
    bz_ref[...] = bz.astype(bz_ref.dtype)
    for val, ref in ((q, q_ref), (k, k_ref), (v, v_ref)):
        for hp in range(N_PAIRS):
            blk = val[:, hp * 128:(hp + 1) * 128].reshape(tm // N_CLASS, N_CLASS, 128)
            ref[hp] = jnp.swapaxes(blk, 0, 1).astype(BF16)
    kt_ref[...] = k.T.reshape(N_HEADS, HEAD_DIM, tm)
    vt_ref[...] = v.T.reshape(N_HEADS, HEAD_DIM, tm)
    os_ref[...] = _attn_sample_body(*sample_refs, mxu_tiles=False)


def _inproj_sample_kernel(x_ref, nw_ref, win_ref, lnw_ref, lnb_ref, coef_ref, mixb_ref, ga_ref,
                          ya_ref, q_ref, k_ref, v_ref, vn_ref, bz_ref):
    ya, vn, q, k, v, bz = _inproj_core(
        x_ref, nw_ref, win_ref, lnw_ref, lnb_ref, ga_ref,
        functools.partial(_mix_new_tokens, coef_ref=coef_ref, mixb_ref=mixb_ref))
    ya_ref[...] = ya
    q_ref[...] = q
    k_ref[...] = k
    v_ref[...] = v
    vn_ref[...] = vn
    bz_ref[...] = bz.astype(bz_ref.dtype)


def _inproj_prompt(x, weights, layer, kt_stack, vt_stack, sample_ops):
    b, s, _ = x.shape
    tm = INPROJ_TILE
    nj = s // tm
    n_rows = b * nj
    tok = lambda w: pl.BlockSpec((None, tm, w), lambda i, j: (i, j, 0))
    hp_spec = pl.BlockSpec((None, N_PAIRS, N_CLASS, tm // N_CLASS, 128), lambda i, j: (i, 0, 0, j, 0))
    t_spec = pl.BlockSpec((None, None, N_HEADS, HEAD_DIM, tm), lambda i, j: (layer, i, 0, 0, j))
    hp_shape = jax.ShapeDtypeStruct((b, N_PAIRS, N_CLASS, s // N_CLASS, 128), BF16)
    t_shape = jax.ShapeDtypeStruct((DEPTH, b, N_HEADS, HEAD_DIM, s), F32)
    s_in, s_out = _sample_specs(sample_ops, layer, lambda i, j: i * nj + j)
    in_specs = ([tok(D_MODEL), _full_spec((1, D_MODEL)), _full_spec((D_MODEL, 7 * WIDTH)),
                 _full_spec((1, WIDTH)), _full_spec((1, WIDTH)),
                 _full_spec((N_GROUPS, CHUNK, CHUNK)), _full_spec((N_GROUPS, CHUNK, 1)),
                 _full_spec((1, WIDTH))] + s_in)
    args = (x,) + tuple(weights) + tuple(sample_ops)
    assert len(args) == N_W + N_S
    aliases = {}
    if kt_stack is not None:
        in_specs = in_specs + [pl.BlockSpec(memory_space=pl.ANY)] * 2
        aliases = {len(args): 4, len(args) + 1: 5}
        args = args + (kt_stack, vt_stack)
    return pl.pallas_call(
        _inproj_prompt_kernel,
        grid=(b, nj),
        in_specs=in_specs,
        out_specs=[tok(WIDTH), hp_spec, hp_spec, hp_spec, t_spec, t_spec, tok(WIDTH), s_out],
        out_shape=[jax.ShapeDtypeStruct((b, s, WIDTH), BF16), hp_shape, hp_shape, hp_shape,
                   t_shape, t_shape, jax.ShapeDtypeStruct((b, s, WIDTH), BF16),
                   jax.ShapeDtypeStruct((n_rows, ROWS, WIDTH), F32)],
        input_output_aliases=aliases,
        compiler_params=pltpu.CompilerParams(
            dimension_semantics=("parallel", "parallel"), vmem_limit_bytes=VMEM_LIMIT),
        name="inproj_prompt",
    )(*args)


def _inproj_sample(x, weights):
    n = x.shape[0]
    full = lambda w: pl.BlockSpec((n, w), lambda i: (0, 0))
    sd = lambda dt: jax.ShapeDtypeStruct((n, WIDTH), dt)
    return pl.pallas_call(
        _inproj_sample_kernel,
        grid=(1,),
        in_specs=[full(D_MODEL), _full_spec((1, D_MODEL)), _full_spec((D_MODEL, 7 * WIDTH)),
                  _full_spec((1, WIDTH)), _full_spec((1, WIDTH)),
                  _full_spec((N_GROUPS, N_NEW, n, 1)), _full_spec((N_GROUPS, n, 1)),
                  _full_spec((1, WIDTH))],
        out_specs=[full(WIDTH)] * 6,
        out_shape=[sd(BF16), sd(F32), sd(F32), sd(F32), sd(F32), sd(BF16)],
        compiler_params=pltpu.CompilerParams(
            dimension_semantics=("arbitrary",), vmem_limit_bytes=VMEM_LIMIT),
        name="inproj_sample",
    )(x, *weights)


def _outproj_kernel(x_ref, ya_ref, o_ref, bz_ref, gb_ref, wout_ref, fw_ref, y_ref, *, final):
    if len(o_ref.shape) == 3:
        o = jnp.concatenate([o_ref[hp] for hp in range(N_PAIRS)], axis=1).astype(F32)
    else:
        o = o_ref[...].astype(F32)
    yb = _rms(o * _silu(bz_ref[...].astype(F32)), gb_ref[...]).astype(BF16)
    ycat = jnp.concatenate([ya_ref[...], yb], axis=1)
    y = x_ref[...] + jnp.dot(ycat, wout_ref[...], preferred_element_type=F32)
    if final:
        y = _rms(y, fw_ref[...])
    y_ref[...] = y


def _outproj(x, ya, o, bz, gb, wout, fw, final):
    b, s, _ = x.shape
    tm = min(OUTPROJ_TILE, s)
    tok = lambda w: pl.BlockSpec((None, tm, w), lambda i, j: (i, j, 0))
    o_spec = tok(WIDTH) if o.ndim == 3 else pl.BlockSpec((None, N_PAIRS, tm, 128), lambda i, j: (i, 0, j, 0))
    return pl.pallas_call(
        functools.partial(_outproj_kernel, final=final),
        grid=(b, s // tm),
        in_specs=[tok(D_MODEL), tok(WIDTH), o_spec,
                  tok(WIDTH), _full_spec((1, WIDTH)), _full_spec((2 * WIDTH, D_MODEL)),
                  _full_spec((1, D_MODEL))],
        out_specs=tok(D_MODEL),
        out_shape=jax.ShapeDtypeStruct((b, s, D_MODEL), F32),
        compiler_params=pltpu.CompilerParams(
            dimension_semantics=("parallel", "parallel"), vmem_limit_bytes=VMEM_LIMIT),
        name="outproj",
    )(x, ya, o, bz, gb, wout, fw)


def _prompt_bucket_index():
    qi = np.arange(BLK)[:, None]
    kj = np.arange(2 * BLK)[None, :]
    part, kk = kj // BLK, kj % BLK
    d1 = 16 * (8 * (1 - part) + qi % 8 - kk % 8) + (qi // 8 - kk // 8)
    ok1 = (d1 >= 0) & (d1 <= SPAN)
    j4 = 4 * (32 * (1 - part) + qi % 32 - kk % 32) + (qi // 32 - kk // 32)
    ok4 = (j4 >= 0) & (j4 <= SPAN)
    j16 = qi - kk + 0 * part
    ok16 = (part == 1) & (j16 >= 0)
    dist = np.stack([d1, 4 * j4, 16 * j16])
    ok = np.stack([ok1, ok4, ok16])
    return np.where(ok, _t5_bucket(np.clip(dist, 0, None)), -1)


def _prompt_units(nrow):
    units = [(2, [(c, 0, BLK)], []) for c in range(N_CLASS)]
    for r in range(4):
        cls = [r + 4 * a for a in range(4)]
        for i in range(nrow // 32):
            units.append((1, [(c, 32 * i, 32) for c in cls], [(c, 32 * i - 32, 32) for c in cls] if i else []))
    allc = list(range(N_CLASS))
    for i in range(nrow // 8):
        units.append((0, [(c, 8 * i, 8) for c in allc], [(c, 8 * i - 8, 8) for c in allc] if i else []))
    return units


def _attn_prompt_kernel(*refs):
    q_ref, k_ref, v_ref, bias_ref = refs[:4]
    sample_refs = refs[4:4 + N_S]
    o_ref, os_ref, qs, ks, vs, s16_scr, s_scr, mg_scr, acc_o, acc_l, onat = refs[4 + N_S:]
    nrow = q_ref.shape[1]
    lane = lax.broadcasted_iota(jnp.int32, (BLK, 128), 1)
    first_head = lane < HEAD_DIM
    units = _prompt_units(nrow)
    phase = pl.program_id(2)

    def gather(ref, lead, pieces):
        return jnp.concatenate([ref[lead + (c, slice(r0, r0 + nr), slice(None))] for c, r0, nr in pieces],
                               axis=0)

    def scatter(ref, lead, pieces, val):
        off = 0
        for c, r0, nr in pieces:
            ref[lead + (c, slice(r0, r0 + nr), slice(None))] = val[off:off + nr]
            off += nr

    def score_slot(u):
        return (s16_scr, u) if u < N_CLASS else (s_scr, u - N_CLASS)

    @pl.when(phase == 0)
    def _scores():
        os_ref[...] = _attn_sample_body(*sample_refs, mxu_tiles=True)
        for c in range(N_CLASS):
            qs[c] = q_ref[c].astype(F32)
            ks[c] = k_ref[c].astype(F32)
            vs[c] = v_ref[c].astype(F32)
        for u, (cfg, q_pieces, prev_pieces) in enumerate(units):
            nk = BLK * (2 if prev_pieces else 1)
            q = gather(qs, (), q_pieces)
            q2 = jnp.concatenate([jnp.where(first_head, q, 0.0), jnp.where(first_head, 0.0, q)], axis=0)
            k = gather(ks, (), prev_pieces + q_pieces).astype(BF16)
            s = lax.dot_general(q2.astype(BF16), k, (((1,), (1,)), ((), ())), preferred_element_type=F32)
            s = s + bias_ref[cfg, :, 2 * BLK - nk:]
            ref, slot = score_slot(u)
            ref[slot, :, :nk] = s
            m = jnp.broadcast_to(jnp.max(s, axis=1, keepdims=True), (2 * BLK, 128))
            for h in range(2):
                mh = m[h * BLK:(h + 1) * BLK]
                if cfg != 2:
                    mh = jnp.maximum(mh, gather(mg_scr, (h,), q_pieces))
                scatter(mg_scr, (h,), q_pieces, mh)

    @pl.when(phase == 1)
    def _softmax():
        os_ref[...] = _attn_sample_body(*sample_refs, mxu_tiles=True)
        for u, (cfg, q_pieces, prev_pieces) in enumerate(units):
            nk = BLK * (2 if prev_pieces else 1)
            v = gather(vs, (), prev_pieces + q_pieces).astype(BF16)
            vext = jnp.concatenate([v, jnp.ones_like(v)], axis=1)
            mq = jnp.concatenate([gather(mg_scr, (0,), q_pieces), gather(mg_scr, (1,), q_pieces)], axis=0)
            if nk > BLK:
                mq = jnp.concatenate([mq, mq], axis=1)
            ref, slot = score_slot(u)
            p = jnp.exp2(ref[slot, :, :nk] - mq).astype(BF16)
            pv = jnp.dot(p, vext, preferred_element_type=F32)
            num = jnp.where(first_head, pv[:BLK, :128], pv[BLK:, :128])
            den = jnp.where(first_head, pv[:BLK, 128:], pv[BLK:, 128:])
            if cfg != 2:
                num = num + gather(acc_o, (), q_pieces)
                den = den + gather(acc_l, (), q_pieces)
            scatter(acc_o, (), q_pieces, num)
            scatter(acc_l, (), q_pieces, den)
        for c in range(N_CLASS):
            onat[pl.ds(c, nrow, stride=N_CLASS), :] = acc_o[c] / acc_l[c]
        o_ref[...] = onat[...].astype(o_ref.dtype)


def _attn_prompt(q_hp, k_hp, v_hp, bias_tab, sample_ops, layer, row0):
    b, _, _, nrow, _ = q_hp.shape
    s = nrow * N_CLASS
    n_rows = b * N_PAIRS * ATTN_PHASES
    blk = pl.BlockSpec((None, None, N_CLASS, nrow, 128), lambda i, j, r: (i, j, 0, 0, 0))
    out_blk = pl.BlockSpec((None, None, s, 128), lambda i, j, r: (i, j, 0, 0))
    cls = lambda lead: pltpu.VMEM(lead + (N_CLASS, nrow, 128), F32)
    n_wide = len(_prompt_units(nrow)) - N_CLASS
    s_in, s_out = _sample_specs(sample_ops, layer,
                                lambda i, j, r: row0 + (i * N_PAIRS + j) * ATTN_PHASES + r)
    s_out = pl.BlockSpec((None, ROWS, WIDTH), lambda i, j, r: ((i * N_PAIRS + j) * ATTN_PHASES + r, 0, 0))
    return pl.pallas_call(
        _attn_prompt_kernel,
        grid=(b, N_PAIRS, ATTN_PHASES),
        in_specs=[blk, blk, blk,
                  pl.BlockSpec((3, None, 2 * BLK, 2 * BLK), lambda i, j, r: (0, j, 0, 0))] + s_in,
        out_specs=[out_blk, s_out],
        out_shape=[jax.ShapeDtypeStruct((b, N_PAIRS, s, 128), BF16),
                   jax.ShapeDtypeStruct((n_rows, ROWS, WIDTH), F32)],
        scratch_shapes=[cls(()), cls(()), cls(()),
                        pltpu.VMEM((N_CLASS, 2 * BLK, BLK), F32),
                        pltpu.VMEM((n_wide, 2 * BLK, 2 * BLK), F32),
                        cls((2,)), cls(()), cls(()),
                        pltpu.VMEM((s, 128), F32)],
        compiler_params=pltpu.CompilerParams(
            dimension_semantics=("parallel", "parallel", "arbitrary"), vmem_limit_bytes=VMEM_LIMIT),
        name="attn_prompt",
    )(q_hp, k_hp, v_hp, bias_tab, *sample_ops)


def _new_token_mix_selectors(nb):
    n = nb * N_NEW
    t = np.arange(n) % N_NEW
    sel_w = np.zeros((N_NEW, N_NEW, N_NEW, n), np.float32)
    sel_b = np.zeros((N_NEW, n), np.float32)
    for r in range(n):
        sel_b[t[r], r] = 1.0
        for d in range(t[r] + 1):
            sel_w[d, t[r], t[r] - d, r] = 1.0
    return jnp.asarray(sel_w), jnp.asarray(sel_b)


def kernel(x_prompt, x_sample, cache_k, cache_v, norm_w, w_in, ln_v_w, ln_v_b, w_spatial, b_spatial,
           rel_bias, out_norm_a, out_norm_b, w_out, final_norm_w):
    b, s, _ = x_prompt.shape
    nb, nt, _ = x_sample.shape
    n_s = nb * nt
    rows_inproj = b * (s // INPROJ_TILE)
    assert cache_k.shape[2] == WB and nt == N_NEW and s % (N_CLASS * BLK) == 0
    assert rows_inproj + b * N_PAIRS * ATTN_PHASES == nb

    kt_all = jnp.transpose(cache_k, (0, 1, 3, 4, 2))
    vt_all = jnp.transpose(cache_v, (0, 1, 3, 4, 2))

    bias_prompt = _bias_tables(_prompt_bucket_index(), rel_bias).reshape(3, N_PAIRS, 2 * BLK, 2 * BLK)
    sample_idx, wtab = _sample_tables()
    btab = _bias_tables(sample_idx, rel_bias)[0]
    sel, new = _sample_selectors()
    sel_w, sel_b = _new_token_mix_selectors(nb)
    hi = lax.Precision.HIGHEST

    tril = np.tril(np.ones((CHUNK, CHUNK), np.float32))
    rows8 = lambda a: jnp.concatenate([a.reshape(nb, nt, WIDTH)] * (ROWS // nt), axis=1)

    xp = x_prompt
    xs = x_sample.reshape(1, n_s, D_MODEL)
    kt_stack = vt_stack = None
    sk, sv, sc = [], [], []
    for l in range(DEPTH):
        win = w_in[l].astype(BF16)
        wout = w_out[l].astype(BF16)
        common = (norm_w[l][None], win, ln_v_w[l][None], ln_v_b[l][None])
        mix_p = (w_spatial[l] * tril).astype(BF16)
        mixb_p = b_spatial[l][:, :, None]
        coef_s = jnp.einsum('gts,dtsr->gdr', w_spatial[l][:, :nt, :nt], sel_w, precision=hi)[..., None]
        mixb_s = jnp.einsum('gt,tr->gr', b_spatial[l][:, :nt], sel_b, precision=hi)[..., None]
        ga, gb = out_norm_a[l][None], out_norm_b[l][None]
        fw = final_norm_w[None]
        final = l == DEPTH - 1

        ya_s, q_s, k_s, v_s, vn_s, bz_s = _inproj_sample(xs[0], common + (coef_s, mixb_s, ga))
        sample_ops = (rows8(q_s), rows8(k_s), rows8(v_s), kt_all, vt_all, sel, new, btab, wtab)
        ya, q_hp, k_hp, v_hp, kt_stack, vt_stack, bz, o_t0 = _inproj_prompt(
            xp, common + (mix_p, mixb_p, ga), l, kt_stack, vt_stack, sample_ops)
        o_hp, o_t1 = _attn_prompt(q_hp, k_hp, v_hp, bias_prompt, sample_ops, l, rows_inproj)
        xp = _outproj(xp, ya, o_hp, bz, gb, wout, fw, final)

        o_s = jnp.concatenate([o_t0, o_t1], axis=0)[:, :nt, :].reshape(1, n_s, WIDTH)
        xs = _outproj(xs, ya_s[None], o_s, bz_s[None], gb, wout, fw, final)
        sk.append(k_s)
        sv.append(v_s)
        sc.append(vn_s)

    heads = (N_HEADS, HEAD_DIM)
    new_k_prompt = jnp.transpose(kt_stack, (0, 1, 4, 2, 3))
    new_v_prompt = jnp.transpose(vt_stack, (0, 1, 4, 2, 3))
    new_k_sample = jnp.stack(sk).reshape((DEPTH, nb, nt) + heads)
    new_v_sample = jnp.stack(sv).reshape((DEPTH, nb, nt) + heads)
    new_vchunk = jnp.stack(sc).reshape(DEPTH, nb, nt, WIDTH)
    return (xp, xs.reshape(nb, nt, D_MODEL), new_k_prompt, new_v_prompt,
            new_k_sample, new_v_sample, new_vchunk)
```

```python
import functools

import numpy as np
import jax
import jax.numpy as jnp
from jax import lax
from jax.experimental import pallas as pl
from jax.experimental.pallas import tpu as pltpu

D_MODEL = 1024
DEPTH = 4
WIDTH = 512
N_GROUPS = 4
N_HEADS = 8
HEAD_DIM = 64
N_PAIRS = 4
CHUNK = 128
SPAN = 128
REL_BUCKETS = 32
REL_MAX_DIST = 2048
ATTN_SCALE = 0.125
LOG2E = 1.4426950408889634
Q_SCALE = ATTN_SCALE * LOG2E
EPS = 1e-6
NEG = -1e30
N_CLASS = 16
BLK = 128
INPROJ_TILE = 256
OUTPROJ_TILE = 1024
ATTN_PHASES = 2
VMEM_LIMIT = 56 * 1024 * 1024

F32 = jnp.float32
BF16 = jnp.bfloat16


def _t5_bucket(dist):
    max_exact = REL_BUCKETS // 2
    large = max_exact + (np.log(np.maximum(dist, 1).astype(np.float32) / max_exact)
                         / np.log(REL_MAX_DIST / max_exact) * (REL_BUCKETS - max_exact)).astype(np.int32)
    large = np.minimum(large, REL_BUCKETS - 1)
    return np.where(dist < max_exact, dist, large).astype(np.int32)


def _rms(x, w):
    return x * lax.rsqrt(jnp.mean(x * x, axis=-1, keepdims=True) + EPS) * w


def _silu(z):
    return z / (1.0 + jnp.exp(-z))


def _full_spec(shape):
    return pl.BlockSpec(shape, lambda *_: (0,) * len(shape))


def _layer_spec(shape, layer):
    return pl.BlockSpec((None,) + shape, lambda *_: (layer,) + (0,) * len(shape))


def _bias_table_kernel(idx_ref, rb_ref, out_ref):
    idx = idx_ref[...]
    for h in range(N_HEADS):
        acc = jnp.full(idx.shape, NEG, F32)
        for bucket in range(REL_BUCKETS):
            acc = jnp.where(idx == bucket, rb_ref[bucket, h] * LOG2E, acc)
        out_ref[h] = acc


def _bias_tables(idx, rel_bias):
    n, r, c = idx.shape
    return pl.pallas_call(
        _bias_table_kernel,
        grid=(n,),
        in_specs=[pl.BlockSpec((None, r, c), lambda i: (i, 0, 0)),
                  pl.BlockSpec(memory_space=pltpu.SMEM)],
        out_specs=pl.BlockSpec((None, N_HEADS, r, c), lambda i: (i, 0, 0, 0)),
        out_shape=jax.ShapeDtypeStruct((n, N_HEADS, r, c), F32),
        compiler_params=pltpu.CompilerParams(dimension_semantics=("parallel",)),
        name="bias_tables",
    )(jnp.asarray(idx, jnp.int32), rel_bias.astype(F32))


N_NEW = 4
ROWS = 8
WB = 2048
N_TILES = WB // 128
NPOS = WB + 128
FAR_TILES = 12
NEAR_TILES = 3
N_PAT = 2 + N_NEW
N_S = 9


def _sample_tables():
    r = np.arange(ROWS)[:, None]
    p = np.arange(NPOS)[None, :]
    i = r % N_NEW
    dist = np.where(p < WB, WB + i - p, i - (p - WB))
    ok = (dist >= 0) & ((p < WB) | (p - WB < N_NEW))
    mult = ((dist <= 128).astype(np.int32) + ((dist % 4 == 0) & (dist <= 512))
            + ((dist % 16 == 0) & (dist <= 2048))) * ok
    idx = np.where(mult > 0, _t5_bucket(np.clip(dist, 0, None)), -1)
    return idx[None], jnp.asarray(mult, F32)


def _sample_selectors():
    lane = np.arange(128)
    sel = np.zeros((ROWS, N_PAT * 128), np.float32)
    new = np.zeros((ROWS, 128), np.float32)
    for i in range(N_NEW):
        sel[i, lane[lane % 16 == i]] = 1.0
        sel[i, 128 + lane[lane % 4 == i]] = 1.0
        sel[i, (2 + i) * 128:(3 + i) * 128] = 1.0
        new[i, i] = 1.0
    return jnp.asarray(sel, BF16), jnp.asarray(new, BF16)


def _sample_specs(sample_ops, layer, row_of):
    sel, new, btab, wtab = sample_ops[5:]
    rows = pl.BlockSpec((None, ROWS, WIDTH), lambda *g: (row_of(*g), 0, 0))
    buf = pl.BlockSpec((None, None, N_HEADS, HEAD_DIM, WB), lambda *g: (layer, row_of(*g), 0, 0, 0))
    in_specs = [rows, rows, rows, buf, buf, _full_spec(sel.shape), _full_spec(new.shape),
                _full_spec(btab.shape), _full_spec(wtab.shape)]
    out_spec = pl.BlockSpec((None, ROWS, WIDTH), lambda *g: (row_of(*g), 0, 0))
    return in_specs, out_spec


def _attn_sample_body(q_ref, k_ref, v_ref, kt_ref, vt_ref, sel_ref, new_ref, b_ref, w_ref, *, mxu_tiles):
    tn = (((0,), (0,)), ((), ()))
    nt = (((1,), (1,)), ((), ()))
    row = lax.broadcasted_iota(jnp.int32, (1, ROWS, 128), 1)
    lane = lax.broadcasted_iota(jnp.int32, (1, 128), 1)
    shape3 = (N_HEADS, ROWS, 128)

    def selected(ref, sel, j):
        t = lax.dot_general(ref[...].astype(BF16), sel, tn, preferred_element_type=F32)
        return [t[:, i * 128:(i + 1) * 128].reshape(N_HEADS, HEAD_DIM, 128) for i in range(j)]

    def token_columns(ref):
        xt = ref[...].T
        return [jnp.broadcast_to(xt[:, i:i + 1], (WIDTH, 128)) for i in range(N_NEW)]

    def by_lane(cols, key):
        out = jnp.zeros((WIDTH, 128), F32)
        for i in range(N_NEW):
            out = jnp.where(key == i, cols[i], out)
        return out.reshape(N_HEADS, HEAD_DIM, 128)

    if mxu_tiles:
        pats = selected(q_ref, sel_ref[...], N_PAT)
        pat16, pat4, tok = pats[0], pats[1], pats[2:]
        (ktn,), (vtn,) = selected(k_ref, new_ref[...], 1), selected(v_ref, new_ref[...], 1)
    else:
        q_cols = token_columns(q_ref)
        tok = [c.reshape(N_HEADS, HEAD_DIM, 128) for c in q_cols]
        pat16, pat4 = by_lane(q_cols, lane % 16), by_lane(q_cols, lane % 4)
        ktn, vtn = by_lane(token_columns(k_ref), lane), by_lane(token_columns(v_ref), lane)
    s_tiles = []
    for j in range(FAR_TILES + NEAR_TILES):
        pat = pat16 if j < FAR_TILES else pat4
        kt = kt_ref[:, :, j * 128:(j + 1) * 128]
        s_tiles.append(jnp.broadcast_to(jnp.sum(kt * pat, axis=1, keepdims=True), shape3))
    for src in (kt_ref[:, :, WB - 128:], ktn):
        t = jnp.zeros(shape3, F32)
        for i in range(N_NEW):
            s_i = jnp.sum(src * tok[i], axis=1, keepdims=True)
            t = jnp.where(row % N_NEW == i, s_i, t)
        s_tiles.append(t)
    ntile = len(s_tiles)
    s_tiles = [s_tiles[j] + b_ref[:, :, j * 128:(j + 1) * 128] for j in range(ntile)]
    m = s_tiles[0]
    for t in s_tiles[1:]:
        m = jnp.maximum(m, t)
    m = jnp.max(m, axis=2, keepdims=True)
    p_tiles = [jnp.exp2(s_tiles[j] - m) * w_ref[:, j * 128:(j + 1) * 128][None] for j in range(ntile)]
    l = p_tiles[0]
    for t in p_tiles[1:]:
        l = l + t
    inv = 1.0 / jnp.sum(l, axis=2, keepdims=True)
    p_tiles = [(t * inv).astype(BF16) for t in p_tiles]

    heads = []
    for h in range(N_HEADS):
        p_win = jnp.concatenate([p_tiles[j][h] for j in range(N_TILES)], axis=1)
        vt_h, vn_h = vt_ref[h].astype(BF16), vtn[h].astype(BF16)
        if mxu_tiles:
            o = lax.dot_general(vt_h, p_win, nt, preferred_element_type=F32)
            o = o + lax.dot_general(vn_h, p_tiles[N_TILES][h], nt, preferred_element_type=F32)
        else:
            o = lax.dot_general(p_win, vt_h, nt, preferred_element_type=F32)
            o = o + lax.dot_general(p_tiles[N_TILES][h], vn_h, nt, preferred_element_type=F32)
        heads.append(o)
    return jnp.concatenate(heads, axis=0).T if mxu_tiles else jnp.concatenate(heads, axis=1)


def _inproj_core(x_ref, nw_ref, win_ref, lnw_ref, lnb_ref, ga_ref, mix_fn):
    h = _rms(x_ref[...], nw_ref[...]).astype(BF16)

    def proj(j):
        return jnp.dot(h, win_ref[:, j * WIDTH:(j + 1) * WIDTH], preferred_element_type=F32)

    a_v = proj(1)
    mu = jnp.mean(a_v, axis=-1, keepdims=True)
    xc = a_v - mu
    vn = xc * lax.rsqrt(jnp.mean(xc * xc, axis=-1, keepdims=True) + EPS) * lnw_ref[...] + lnb_ref[...]
    a_out = proj(0) * mix_fn(vn)
    ya = _rms(a_out * _silu(proj(2)), ga_ref[...]).astype(BF16)
    q = proj(3) * Q_SCALE
    return ya, vn, q, proj(4), proj(5), proj(6)


def _mix_chunks(vn, mix_ref, mixb_ref):
    tm = vn.shape[0]
    chunk = mix_ref.shape[1]
    nch = tm // chunk
    vnb = vn.astype(BF16)
    cols = []
    for g in range(N_GROUPS):
        vg = vnb[:, g * 128:(g + 1) * 128]
        if nch > 1:
            vg = jnp.concatenate([vg[ci * chunk:(ci + 1) * chunk] for ci in range(nch)], axis=1)
        r = jnp.dot(mix_ref[g], vg, preferred_element_type=F32) + mixb_ref[g]
        if nch > 1:
            r = jnp.concatenate([r[:, ci * 128:(ci + 1) * 128] for ci in range(nch)], axis=0)
        cols.append(r)
    return jnp.concatenate(cols, axis=1)


def _mix_new_tokens(vn, coef_ref, mixb_ref):
    cols = []
    for g in range(N_GROUPS):
        vg = vn[:, g * 128:(g + 1) * 128]
        acc = coef_ref[g, 0] * vg + mixb_ref[g]
        for d in range(1, N_NEW):
            acc = acc + coef_ref[g, d] * pltpu.roll(vg, d, axis=0)
        cols.append(acc)
    return jnp.concatenate(cols, axis=1)


N_W = 8


def _inproj_prompt_kernel(*refs):
    x_ref, nw_ref, win_ref, lnw_ref, lnb_ref, mix_ref, mixb_ref, ga_ref = refs[:N_W]
    sample_refs = refs[N_W:N_W + N_S]
    ya_ref, q_ref, k_ref, v_ref, kt_ref, vt_ref, bz_ref, os_ref = refs[-8:]
    tm = x_ref.shape[0]
    ya, _, q, k, v, bz = _inproj_core(
        x_ref, nw_ref, win_ref, lnw_ref, lnb_ref, ga_ref,
        functools.partial(_mix_chunks, mix_ref=mix_ref, mixb_ref=mixb_ref))
    ya_ref[...] = ya
    bz_ref[...] = bz.astype(bz_ref.dtype)
    for val, ref in ((q, q_ref), (k, k_ref), (v, v_ref)):
        for hp in range(N_PAIRS):
            blk = val[:, hp * 128:(hp + 1) * 128].reshape(tm // N_CLASS, N_CLASS, 128)
            ref[hp] = jnp.swapaxes(blk, 0, 1).astype(BF16)
    kt_ref[...] = k.T.reshape(N_HEADS, HEAD_DIM, tm)
    vt_ref[...] = v.T.reshape(N_HEADS, HEAD_DIM, tm)
    os_ref[...] = _attn_sample_body(*sample_refs, mxu_tiles=False)


def _inproj_sample_kernel(x_ref, nw_ref, win_ref, lnw_ref, lnb_ref, coef_ref, mixb_ref, ga_ref,
                          ya_ref, q_ref, k_ref, v_ref, vn_ref, bz_ref):
    ya, vn, q, k, v, bz = _inproj_core(
        x_ref, nw_ref, win_ref, lnw_ref, lnb_ref, ga_ref,
        functools.partial(_mix_new_tokens, coef_ref=coef_ref, mixb_ref=mixb_ref))
    ya_ref[...] = ya
    q_ref[...] = q
    k_ref[...] = k
    v_ref[...] = v
    vn_ref[...] = vn
    bz_ref[...] = bz.astype(bz_ref.dtype)


def _inproj_prompt(x, weights, layer, kt_stack, vt_stack, sample_ops):
    b, s, _ = x.shape
    tm = INPROJ_TILE
    nj = s // tm
    n_rows = b * nj
    tok = lambda w: pl.BlockSpec((None, tm, w), lambda i, j: (i, j, 0))
    hp_spec = pl.BlockSpec((None, N_PAIRS, N_CLASS, tm // N_CLASS, 128), lambda i, j: (i, 0, 0, j, 0))
    t_spec = pl.BlockSpec((None, None, N_HEADS, HEAD_DIM, tm), lambda i, j: (layer, i, 0, 0, j))
    hp_shape = jax.ShapeDtypeStruct((b, N_PAIRS, N_CLASS, s // N_CLASS, 128), BF16)
    t_shape = jax.ShapeDtypeStruct((DEPTH, b, N_HEADS, HEAD_DIM, s), F32)
    s_in, s_out = _sample_specs(sample_ops, layer, lambda i, j: i * nj + j)
    in_specs = ([tok(D_MODEL), _full_spec((1, D_MODEL)), _layer_spec((D_MODEL, 7 * WIDTH), layer),
                 _full_spec((1, WIDTH)), _full_spec((1, WIDTH)),
                 _full_spec((N_GROUPS, CHUNK, CHUNK)), _full_spec((N_GROUPS, CHUNK, 1)),
                 _full_spec((1, WIDTH))] + s_in)
    args = (x,) + tuple(weights) + tuple(sample_ops)
    assert len(args) == N_W + N_S
    aliases = {}
    if kt_stack is not None:
        in_specs = in_specs + [pl.BlockSpec(memory_space=pl.ANY)] * 2
        aliases = {len(args): 4, len(args) + 1: 5}
        args = args + (kt_stack, vt_stack)
    return pl.pallas_call(
        _inproj_prompt_kernel,
        grid=(b, nj),
        in_specs=in_specs,
        out_specs=[tok(WIDTH), hp_spec, hp_spec, hp_spec, t_spec, t_spec, tok(WIDTH), s_out],
        out_shape=[jax.ShapeDtypeStruct((b, s, WIDTH), BF16), hp_shape, hp_shape, hp_shape,
                   t_shape, t_shape, jax.ShapeDtypeStruct((b, s, WIDTH), BF16),
                   jax.ShapeDtypeStruct((n_rows, ROWS, WIDTH), F32)],
        input_output_aliases=aliases,
        compiler_params=pltpu.CompilerParams(
            dimension_semantics=("parallel", "parallel"), vmem_limit_bytes=VMEM_LIMIT),
        name="inproj_prompt",
    )(*args)


def _inproj_sample(x, weights, layer):
    n = x.shape[0]
    full = lambda w: pl.BlockSpec((n, w), lambda i: (0, 0))
    sd = lambda dt: jax.ShapeDtypeStruct((n, WIDTH), dt)
    return pl.pallas_call(
        _inproj_sample_kernel,
        grid=(1,),
        in_specs=[full(D_MODEL), _full_spec((1, D_MODEL)), _layer_spec((D_MODEL, 7 * WIDTH), layer),
                  _full_spec((1, WIDTH)), _full_spec((1, WIDTH)),
                  _full_spec((N_GROUPS, N_NEW, n, 1)), _full_spec((N_GROUPS, n, 1)),
                  _full_spec((1, WIDTH))],
        out_specs=[full(WIDTH)] * 6,
        out_shape=[sd(BF16), sd(F32), sd(F32), sd(F32), sd(F32), sd(BF16)],
        compiler_params=pltpu.CompilerParams(
            dimension_semantics=("arbitrary",), vmem_limit_bytes=VMEM_LIMIT),
        name="inproj_sample",
    )(x, *weights)


def _outproj_kernel(x_ref, ya_ref, o_ref, bz_ref, gb_ref, wout_ref, fw_ref, y_ref, *, final):
    if len(o_ref.shape) == 3:
        o = jnp.concatenate([o_ref[hp] for hp in range(N_PAIRS)], axis=1).astype(F32)
    else:
        o = o_ref[...].astype(F32)
    yb = _rms(o * _silu(bz_ref[...].astype(F32)), gb_ref[...]).astype(BF16)
    ycat = jnp.concatenate([ya_ref[...], yb], axis=1)
    y = x_ref[...] + jnp.dot(ycat, wout_ref[...], preferred_element_type=F32)
    if final:
        y = _rms(y, fw_ref[...])
    y_ref[...] = y


def _outproj(x, ya, o, bz, gb, wout, fw, layer, final):
    b, s, _ = x.shape
    tm = min(OUTPROJ_TILE, s)
    tok = lambda w: pl.BlockSpec((None, tm, w), lambda i, j: (i, j, 0))
    o_spec = tok(WIDTH) if o.ndim == 3 else pl.BlockSpec((None, N_PAIRS, tm, 128), lambda i, j: (i, 0, j, 0))
    return pl.pallas_call(
        functools.partial(_outproj_kernel, final=final),
        grid=(b, s // tm),
        in_specs=[tok(D_MODEL), tok(WIDTH), o_spec,
                  tok(WIDTH), _full_spec((1, WIDTH)), _layer_spec((2 * WIDTH, D_MODEL), layer),
                  _full_spec((1, D_MODEL))],
        out_specs=tok(D_MODEL),
        out_shape=jax.ShapeDtypeStruct((b, s, D_MODEL), F32),
        compiler_params=pltpu.CompilerParams(
            dimension_semantics=("parallel", "parallel"), vmem_limit_bytes=VMEM_LIMIT),
        name="outproj",
    )(x, ya, o, bz, gb, wout, fw)


def _prompt_bucket_index():
    qi = np.arange(BLK)[:, None]
    kj = np.arange(2 * BLK)[None, :]
    part, kk = kj // BLK, kj % BLK
    d1 = 16 * (8 * (1 - part) + qi % 8 - kk % 8) + (qi // 8 - kk // 8)
    ok1 = (d1 >= 0) & (d1 <= SPAN)
    j4 = 4 * (32 * (1 - part) + qi % 32 - kk % 32) + (qi // 32 - kk // 32)
    ok4 = (j4 >= 0) & (j4 <= SPAN)
    j16 = qi - kk + 0 * part
    ok16 = (part == 1) & (j16 >= 0)
    dist = np.stack([d1, 4 * j4, 16 * j16])
    ok = np.stack([ok1, ok4, ok16])
    return np.where(ok, _t5_bucket(np.clip(dist, 0, None)), -1)


def _prompt_units(nrow):
    units = [(2, [(c, 0, BLK)], []) for c in range(N_CLASS)]
    for r in range(4):
        cls = [r + 4 * a for a in range(4)]
        for i in range(nrow // 32):
            units.append((1, [(c, 32 * i, 32) for c in cls], [(c, 32 * i - 32, 32) for c in cls] if i else []))
    allc = list(range(N_CLASS))
    for i in range(nrow // 8):
        units.append((0, [(c, 8 * i, 8) for c in allc], [(c, 8 * i - 8, 8) for c in allc] if i else []))
    return units


def _attn_prompt_kernel(*refs):
    q_ref, k_ref, v_ref, bias_ref = refs[:4]
    sample_refs = refs[4:4 + N_S]
    o_ref, os_ref, qs, ks, vs, s16_scr, s_scr, mg_scr, acc_o, acc_l, onat = refs[4 + N_S:]
    nrow = q_ref.shape[1]
    lane = lax.broadcasted_iota(jnp.int32, (BLK, 128), 1)
    first_head = lane < HEAD_DIM
    units = _prompt_units(nrow)
    phase = pl.program_id(2)

    def gather(ref, lead, pieces):
        return jnp.concatenate([ref[lead + (c, slice(r0, r0 + nr), slice(None))] for c, r0, nr in pieces],
                               axis=0)

    def scatter(ref, lead, pieces, val):
        off = 0
        for c, r0, nr in pieces:
            ref[lead + (c, slice(r0, r0 + nr), slice(None))] = val[off:off + nr]
            off += nr

    def score_slot(u):
        return (s16_scr, u) if u < N_CLASS else (s_scr, u - N_CLASS)

    @pl.when(phase == 0)
    def _scores():
        os_ref[...] = _attn_sample_body(*sample_refs, mxu_tiles=True)
        for c in range(N_CLASS):
            qs[c] = q_ref[c].astype(F32)
            ks[c] = k_ref[c].astype(F32)
            vs[c] = v_ref[c].astype(F32)
        for u, (cfg, q_pieces, prev_pieces) in enumerate(units):
            nk = BLK * (2 if prev_pieces else 1)
            q = gather(qs, (), q_pieces)
            q2 = jnp.concatenate([jnp.where(first_head, q, 0.0), jnp.where(first_head, 0.0, q)], axis=0)
            k = gather(ks, (), prev_pieces + q_pieces).astype(BF16)
            s = lax.dot_general(q2.astype(BF16), k, (((1,), (1,)), ((), ())), preferred_element_type=F32)
            s = s + bias_ref[cfg, :, 2 * BLK - nk:]
            ref, slot = score_slot(u)
            ref[slot, :, :nk] = s
            m = jnp.broadcast_to(jnp.max(s, axis=1, keepdims=True), (2 * BLK, 128))
            for h in range(2):
                mh = m[h * BLK:(h + 1) * BLK]
                if cfg != 2:
                    mh = jnp.maximum(mh, gather(mg_scr, (h,), q_pieces))
                scatter(mg_scr, (h,), q_pieces, mh)

    @pl.when(phase == 1)
    def _softmax():
        os_ref[...] = _attn_sample_body(*sample_refs, mxu_tiles=True)
        for u, (cfg, q_pieces, prev_pieces) in enumerate(units):
            nk = BLK * (2 if prev_pieces else 1)
            v = gather(vs, (), prev_pieces + q_pieces).astype(BF16)
            vext = jnp.concatenate([v, jnp.ones_like(v)], axis=1)
            mq = jnp.concatenate([gather(mg_scr, (0,), q_pieces), gather(mg_scr, (1,), q_pieces)], axis=0)
            if nk > BLK:
                mq = jnp.concatenate([mq, mq], axis=1)
            ref, slot = score_slot(u)
            p = jnp.exp2(ref[slot, :, :nk] - mq).astype(BF16)
            pv = jnp.dot(p, vext, preferred_element_type=F32)
            num = jnp.where(first_head, pv[:BLK, :128], pv[BLK:, :128])
            den = jnp.where(first_head, pv[:BLK, 128:], pv[BLK:, 128:])
            if cfg != 2:
                num = num + gather(acc_o, (), q_pieces)
                den = den + gather(acc_l, (), q_pieces)
            scatter(acc_o, (), q_pieces, num)
            scatter(acc_l, (), q_pieces, den)
        for c in range(N_CLASS):
            onat[pl.ds(c, nrow, stride=N_CLASS), :] = acc_o[c] / acc_l[c]
        o_ref[...] = onat[...].astype(o_ref.dtype)


def _attn_prompt(q_hp, k_hp, v_hp, bias_tab, sample_ops, layer, row0):
    b, _, _, nrow, _ = q_hp.shape
    s = nrow * N_CLASS
    n_rows = b * N_PAIRS * ATTN_PHASES
    blk = pl.BlockSpec((None, None, N_CLASS, nrow, 128), lambda i, j, r: (i, j, 0, 0, 0))
    out_blk = pl.BlockSpec((None, None, s, 128), lambda i, j, r: (i, j, 0, 0))
    cls = lambda lead: pltpu.VMEM(lead + (N_CLASS, nrow, 128), F32)
    n_wide = len(_prompt_units(nrow)) - N_CLASS
    s_in, s_out = _sample_specs(sample_ops, layer,
                                lambda i, j, r: row0 + (i * N_PAIRS + j) * ATTN_PHASES + r)
    s_out = pl.BlockSpec((None, ROWS, WIDTH), lambda i, j, r: ((i * N_PAIRS + j) * ATTN_PHASES + r, 0, 0))
    return pl.pallas_call(
        _attn_prompt_kernel,
        grid=(b, N_PAIRS, ATTN_PHASES),
        in_specs=[blk, blk, blk,
                  pl.BlockSpec((3, None, 2 * BLK, 2 * BLK), lambda i, j, r: (0, j, 0, 0))] + s_in,
        out_specs=[out_blk, s_out],
        out_shape=[jax.ShapeDtypeStruct((b, N_PAIRS, s, 128), BF16),
                   jax.ShapeDtypeStruct((n_rows, ROWS, WIDTH), F32)],
        scratch_shapes=[cls(()), cls(()), cls(()),
                        pltpu.VMEM((N_CLASS, 2 * BLK, BLK), F32),
                        pltpu.VMEM((n_wide, 2 * BLK, 2 * BLK), F32),
                        cls((2,)), cls(()), cls(()),
                        pltpu.VMEM((s, 128), F32)],
        compiler_params=pltpu.CompilerParams(
            dimension_semantics=("parallel", "parallel", "arbitrary"), vmem_limit_bytes=VMEM_LIMIT),
        name="attn_prompt",
    )(q_hp, k_hp, v_hp, bias_tab, *sample_ops)


def _new_token_mix_selectors(nb):
    n = nb * N_NEW
    t = np.arange(n) % N_NEW
    sel_w = np.zeros((N_NEW, N_NEW, N_NEW, n), np.float32)
    sel_b = np.zeros((N_NEW, n), np.float32)
    for r in range(n):
        sel_b[t[r], r] = 1.0
        for d in range(t[r] + 1):
            sel_w[d, t[r], t[r] - d, r] = 1.0
    return jnp.asarray(sel_w), jnp.asarray(sel_b)


def kernel(x_prompt, x_sample, cache_k, cache_v, norm_w, w_in, ln_v_w, ln_v_b, w_spatial, b_spatial,
           rel_bias, out_norm_a, out_norm_b, w_out, final_norm_w):
    b, s, _ = x_prompt.shape
    nb, nt, _ = x_sample.shape
    n_s = nb * nt
    rows_inproj = b * (s // INPROJ_TILE)
    assert cache_k.shape[2] == WB and nt == N_NEW and s % (N_CLASS * BLK) == 0
    assert rows_inproj + b * N_PAIRS * ATTN_PHASES == nb

    kt_all = jnp.transpose(cache_k, (0, 1, 3, 4, 2))
    vt_all = jnp.transpose(cache_v, (0, 1, 3, 4, 2))

    bias_prompt = _bias_tables(_prompt_bucket_index(), rel_bias).reshape(3, N_PAIRS, 2 * BLK, 2 * BLK)
    sample_idx, wtab = _sample_tables()
    btab = _bias_tables(sample_idx, rel_bias)[0]
    sel, new = _sample_selectors()
    sel_w, sel_b = _new_token_mix_selectors(nb)
    hi = lax.Precision.HIGHEST

    tril = np.tril(np.ones((CHUNK, CHUNK), np.float32))
    rows8 = lambda a: jnp.concatenate([a.reshape(nb, nt, WIDTH)] * (ROWS // nt), axis=1)

    xp = x_prompt
    xs = x_sample.reshape(1, n_s, D_MODEL)
    kt_stack = vt_stack = None
    sk, sv, sc = [], [], []
    win, wout = w_in.astype(BF16), w_out.astype(BF16)
    for l in range(DEPTH):
        common = (norm_w[l][None], win, ln_v_w[l][None], ln_v_b[l][None])
        mix_p = (w_spatial[l] * tril).astype(BF16)
        mixb_p = b_spatial[l][:, :, None]
        coef_s = jnp.einsum('gts,dtsr->gdr', w_spatial[l][:, :nt, :nt], sel_w, precision=hi)[..., None]
        mixb_s = jnp.einsum('gt,tr->gr', b_spatial[l][:, :nt], sel_b, precision=hi)[..., None]
        ga, gb = out_norm_a[l][None], out_norm_b[l][None]
        fw = final_norm_w[None]
        final = l == DEPTH - 1

        ya_s, q_s, k_s, v_s, vn_s, bz_s = _inproj_sample(xs[0], common + (coef_s, mixb_s, ga), l)
        sample_ops = (rows8(q_s), rows8(k_s), rows8(v_s), kt_all, vt_all, sel, new, btab, wtab)
        ya, q_hp, k_hp, v_hp, kt_stack, vt_stack, bz, o_t0 = _inproj_prompt(
            xp, common + (mix_p, mixb_p, ga), l, kt_stack, vt_stack, sample_ops)
        o_hp, o_t1 = _attn_prompt(q_hp, k_hp, v_hp, bias_prompt, sample_ops, l, rows_inproj)
        xp = _outproj(xp, ya, o_hp, bz, gb, wout, fw, l, final)

        o_s = jnp.concatenate([o_t0, o_t1], axis=0)[:, :nt, :].reshape(1, n_s, WIDTH)
        xs = _outproj(xs, ya_s[None], o_s, bz_s[None], gb, wout, fw, l, final)
        sk.append(k_s)
        sv.append(v_s)
        sc.append(vn_s)

    heads = (N_HEADS, HEAD_DIM)
    new_k_prompt = jnp.transpose(kt_stack, (0, 1, 4, 2, 3))
    new_v_prompt = jnp.transpose(vt_stack, (0, 1, 4, 2, 3))
    new_k_sample = jnp.stack(sk).reshape((DEPTH, nb, nt) + heads)
    new_v_sample = jnp.stack(sv).reshape((DEPTH, nb, nt) + heads)
    new_vchunk = jnp.stack(sc).reshape(DEPTH, nb, nt, WIDTH)
    return (xp, xs.reshape(nb, nt, D_MODEL), new_k_prompt, new_v_prompt,
            new_k_sample, new_v_sample, new_vchunk)
```

```python
import functools

import numpy as np
import jax
import jax.numpy as jnp
from jax import lax
from jax.experimental import pallas as pl
from jax.experimental.pallas import tpu as pltpu

D_MODEL = 1024
DEPTH = 4
WIDTH = 512
N_GROUPS = 4
N_HEADS = 8
HEAD_DIM = 64
N_PAIRS = 4
CHUNK = 128
SPAN = 128
REL_BUCKETS = 32
REL_MAX_DIST = 2048
ATTN_SCALE = 0.125
LOG2E = 1.4426950408889634
Q_SCALE = ATTN_SCALE * LOG2E
EPS = 1e-6
NEG = -1e30
N_CLASS = 16
BLK = 128
INPROJ_TILE = 256
OUTPROJ_TILE = 1024
ATTN_PHASES = 2
VMEM_LIMIT = 56 * 1024 * 1024

F32 = jnp.float32
BF16 = jnp.bfloat16


def _t5_bucket(dist):
    max_exact = REL_BUCKETS // 2
    large = max_exact + (np.log(np.maximum(dist, 1).astype(np.float32) / max_exact)
                         / np.log(REL_MAX_DIST / max_exact) * (REL_BUCKETS - max_exact)).astype(np.int32)
    large = np.minimum(large, REL_BUCKETS - 1)
    return np.where(dist < max_exact, dist, large).astype(np.int32)


def _rms(x, w):
    return x * lax.rsqrt(jnp.mean(x * x, axis=-1, keepdims=True) + EPS) * w


def _silu(z):
    return z / (1.0 + jnp.exp(-z))


def _full_spec(shape):
    return pl.BlockSpec(shape, lambda *_: (0,) * len(shape))


def _layer_spec(shape, layer):
    return pl.BlockSpec((None,) + shape, lambda *_: (layer,) + (0,) * len(shape))


def _bias_table_kernel(idx_ref, rb_ref, out_ref):
    idx = idx_ref[...]
    for h in range(N_HEADS):
        acc = jnp.full(idx.shape, NEG, F32)
        for bucket in range(REL_BUCKETS):
            acc = jnp.where(idx == bucket, rb_ref[bucket, h] * LOG2E, acc)
        out_ref[h] = acc


def _bias_tables(idx, rel_bias):
    n, r, c = idx.shape
    return pl.pallas_call(
        _bias_table_kernel,
        grid=(n,),
        in_specs=[pl.BlockSpec((None, r, c), lambda i: (i, 0, 0)),
                  pl.BlockSpec(memory_space=pltpu.SMEM)],
        out_specs=pl.BlockSpec((None, N_HEADS, r, c), lambda i: (i, 0, 0, 0)),
        out_shape=jax.ShapeDtypeStruct((n, N_HEADS, r, c), F32),
        compiler_params=pltpu.CompilerParams(dimension_semantics=("parallel",)),
        name="bias_tables",
    )(jnp.asarray(idx, jnp.int32), rel_bias.astype(F32))


N_NEW = 4
ROWS = 8
WB = 2048
N_TILES = WB // 128
NPOS = WB + 128
FAR_TILES = 12
NEAR_TILES = 3
N_PAT = 2 + N_NEW
N_S = 7


def _sample_tables():
    r = np.arange(ROWS)[:, None]
    p = np.arange(NPOS)[None, :]
    i = r % N_NEW
    dist = np.where(p < WB, WB + i - p, i - (p - WB))
    ok = (dist >= 0) & ((p < WB) | (p - WB < N_NEW))
    mult = ((dist <= 128).astype(np.int32) + ((dist % 4 == 0) & (dist <= 512))
            + ((dist % 16 == 0) & (dist <= 2048))) * ok
    idx = np.where(mult > 0, _t5_bucket(np.clip(dist, 0, None)), -1)
    return idx[None], jnp.asarray(mult, F32)


def _sample_selectors():
    lane = np.arange(128)
    sel = np.zeros((ROWS, N_PAT * 128), np.float32)
    new = np.zeros((ROWS, 128), np.float32)
    for i in range(N_NEW):
        sel[i, lane[lane % 16 == i]] = 1.0
        sel[i, 128 + lane[lane % 4 == i]] = 1.0
        sel[i, (2 + i) * 128:(3 + i) * 128] = 1.0
        new[i, i] = 1.0
    return jnp.asarray(sel, BF16), jnp.asarray(new, BF16)


def _sample_specs(sample_ops, layer, row_of):
    sel, new, btab, wtab = sample_ops[3:]
    rows = pl.BlockSpec((None, 3, ROWS, WIDTH), lambda *g: (row_of(*g), 0, 0, 0))
    buf = pl.BlockSpec((None, None, N_HEADS, HEAD_DIM, WB), lambda *g: (layer, row_of(*g), 0, 0, 0))
    in_specs = [rows, buf, buf, _full_spec(sel.shape), _full_spec(new.shape),
                _full_spec(btab.shape), _full_spec(wtab.shape)]
    out_spec = pl.BlockSpec((None, ROWS, WIDTH), lambda *g: (row_of(*g), 0, 0))
    return in_specs, out_spec


def _attn_sample_body(qkv_ref, kt_ref, vt_ref, sel_ref, new_ref, b_ref, w_ref, *, mxu_tiles):
    tn = (((0,), (0,)), ((), ()))
    nt = (((1,), (1,)), ((), ()))
    q_ref, k_ref, v_ref = qkv_ref.at[0], qkv_ref.at[1], qkv_ref.at[2]
    row = lax.broadcasted_iota(jnp.int32, (1, ROWS, 128), 1)
    lane = lax.broadcasted_iota(jnp.int32, (1, 128), 1)
    shape3 = (N_HEADS, ROWS, 128)

    def selected(ref, sel, j):
        t = lax.dot_general(ref[...].astype(BF16), sel, tn, preferred_element_type=F32)
        return [t[:, i * 128:(i + 1) * 128].reshape(N_HEADS, HEAD_DIM, 128) for i in range(j)]

    def token_columns(ref):
        xt = ref[...].T
        return [jnp.broadcast_to(xt[:, i:i + 1], (WIDTH, 128)) for i in range(N_NEW)]

    def by_lane(cols, key):
        out = jnp.zeros((WIDTH, 128), F32)
        for i in range(N_NEW):
            out = jnp.where(key == i, cols[i], out)
        return out.reshape(N_HEADS, HEAD_DIM, 128)

    if mxu_tiles:
        pats = selected(q_ref, sel_ref[...], N_PAT)
        pat16, pat4, tok = pats[0], pats[1], pats[2:]
        (ktn,), (vtn,) = selected(k_ref, new_ref[...], 1), selected(v_ref, new_ref[...], 1)
    else:
        q_cols = token_columns(q_ref)
        tok = [c.reshape(N_HEADS, HEAD_DIM, 128) for c in q_cols]
        pat16, pat4 = by_lane(q_cols, lane % 16), by_lane(q_cols, lane % 4)
        ktn, vtn = by_lane(token_columns(k_ref), lane), by_lane(token_columns(v_ref), lane)
    s_tiles = []
    for j in range(FAR_TILES + NEAR_TILES):
        pat = pat16 if j < FAR_TILES else pat4
        kt = kt_ref[:, :, j * 128:(j + 1) * 128]
        s_tiles.append(jnp.broadcast_to(jnp.sum(kt * pat, axis=1, keepdims=True), shape3))
    for src in (kt_ref[:, :, WB - 128:], ktn):
        t = jnp.zeros(shape3, F32)
        for i in range(N_NEW):
            s_i = jnp.sum(src * tok[i], axis=1, keepdims=True)
            t = jnp.where(row % N_NEW == i, s_i, t)
        s_tiles.append(t)
    ntile = len(s_tiles)
    s_tiles = [s_tiles[j] + b_ref[:, :, j * 128:(j + 1) * 128] for j in range(ntile)]
    m = s_tiles[0]
    for t in s_tiles[1:]:
        m = jnp.maximum(m, t)
    m = jnp.max(m, axis=2, keepdims=True)
    p_tiles = [jnp.exp2(s_tiles[j] - m) * w_ref[:, j * 128:(j + 1) * 128][None] for j in range(ntile)]
    l = p_tiles[0]
    for t in p_tiles[1:]:
        l = l + t
    inv = 1.0 / jnp.sum(l, axis=2, keepdims=True)
    p_tiles = [(t * inv).astype(BF16) for t in p_tiles]

    heads = []
    for h in range(N_HEADS):
        p_win = jnp.concatenate([p_tiles[j][h] for j in range(N_TILES)], axis=1)
        vt_h, vn_h = vt_ref[h].astype(BF16), vtn[h].astype(BF16)
        if mxu_tiles:
            o = lax.dot_general(vt_h, p_win, nt, preferred_element_type=F32)
            o = o + lax.dot_general(vn_h, p_tiles[N_TILES][h], nt, preferred_element_type=F32)
        else:
            o = lax.dot_general(p_win, vt_h, nt, preferred_element_type=F32)
            o = o + lax.dot_general(p_tiles[N_TILES][h], vn_h, nt, preferred_element_type=F32)
        heads.append(o)
    return jnp.concatenate(heads, axis=0).T if mxu_tiles else jnp.concatenate(heads, axis=1)


def _inproj_core(x_ref, nw_ref, win_ref, lnw_ref, lnb_ref, ga_ref, mix_fn):
    h = _rms(x_ref[...], nw_ref[...]).astype(BF16)

    def proj(j):
        return jnp.dot(h, win_ref[:, j * WIDTH:(j + 1) * WIDTH], preferred_element_type=F32)

    a_v = proj(1)
    mu = jnp.mean(a_v, axis=-1, keepdims=True)
    xc = a_v - mu
    vn = xc * lax.rsqrt(jnp.mean(xc * xc, axis=-1, keepdims=True) + EPS) * lnw_ref[...] + lnb_ref[...]
    a_out = proj(0) * mix_fn(vn)
    ya = _rms(a_out * _silu(proj(2)), ga_ref[...]).astype(BF16)
    q = proj(3) * Q_SCALE
    return ya, vn, q, proj(4), proj(5), proj(6)


def _mix_chunks(vn, mix_ref, mixb_ref):
    tm = vn.shape[0]
    chunk = mix_ref.shape[1]
    nch = tm // chunk
    vnb = vn.astype(BF16)
    cols = []
    for g in range(N_GROUPS):
        vg = vnb[:, g * 128:(g + 1) * 128]
        if nch > 1:
            vg = jnp.concatenate([vg[ci * chunk:(ci + 1) * chunk] for ci in range(nch)], axis=1)
        r = jnp.dot(mix_ref[g], vg, preferred_element_type=F32) + mixb_ref[g]
        if nch > 1:
            r = jnp.concatenate([r[:, ci * 128:(ci + 1) * 128] for ci in range(nch)], axis=0)
        cols.append(r)
    return jnp.concatenate(cols, axis=1)


def _mix_new_tokens(vn, coef_ref, mixb_ref):
    cols = []
    for g in range(N_GROUPS):
        vg = vn[:, g * 128:(g + 1) * 128]
        acc = coef_ref[g, 0] * vg + mixb_ref[g]
        for d in range(1, N_NEW):
            acc = acc + coef_ref[g, d] * pltpu.roll(vg, d, axis=0)
        cols.append(acc)
    return jnp.concatenate(cols, axis=1)


N_W = 8


def _inproj_prompt_kernel(*refs):
    x_ref, nw_ref, win_ref, lnw_ref, lnb_ref, mix_ref, mixb_ref, ga_ref = refs[:N_W]
    sample_refs = refs[N_W:N_W + N_S]
    ya_ref, qkv_ref, kt_ref, vt_ref, bz_ref, os_ref = refs[-6:]
    tm = x_ref.shape[0]
    ya, _, q, k, v, bz = _inproj_core(
        x_ref, nw_ref, win_ref, lnw_ref, lnb_ref, ga_ref,
        functools.partial(_mix_chunks, mix_ref=mix_ref, mixb_ref=mixb_ref))
    ya_ref[...] = ya
    bz_ref[...] = bz.astype(bz_ref.dtype)
    for t, val in enumerate((q, k, v)):
        for hp in range(N_PAIRS):
            blk = val[:, hp * 128:(hp + 1) * 128].reshape(tm // N_CLASS, N_CLASS, 128)
            qkv_ref[t, hp] = jnp.swapaxes(blk, 0, 1).astype(BF16)
    kt_ref[...] = k.T.reshape(N_HEADS, HEAD_DIM, tm)
    vt_ref[...] = v.T.reshape(N_HEADS, HEAD_DIM, tm)
    os_ref[...] = _attn_sample_body(*sample_refs, mxu_tiles=False)


def _inproj_sample_kernel(x_ref, nw_ref, win_ref, lnw_ref, lnb_ref, coef_ref, mixb_ref, ga_ref,
                          ya_ref, q_ref, k_ref, v_ref, vn_ref, bz_ref):
    ya, vn, q, k, v, bz = _inproj_core(
        x_ref, nw_ref, win_ref, lnw_ref, lnb_ref, ga_ref,
        functools.partial(_mix_new_tokens, coef_ref=coef_ref, mixb_ref=mixb_ref))
    ya_ref[...] = ya
    q_ref[...] = q
    k_ref[...] = k
    v_ref[...] = v
    vn_ref[...] = vn
    bz_ref[...] = bz.astype(bz_ref.dtype)


def _inproj_prompt(x, weights, layer, kt_stack, vt_stack, sample_ops):
    b, s, _ = x.shape
    tm = INPROJ_TILE
    nj = s // tm
    n_rows = b * nj
    tok = lambda w: pl.BlockSpec((None, tm, w), lambda i, j: (i, j, 0))
    hp_spec = pl.BlockSpec((None, 3, N_PAIRS, N_CLASS, tm // N_CLASS, 128), lambda i, j: (i, 0, 0, 0, j, 0))
    t_spec = pl.BlockSpec((None, None, N_HEADS, HEAD_DIM, tm), lambda i, j: (layer, i, 0, 0, j))
    hp_shape = jax.ShapeDtypeStruct((b, 3, N_PAIRS, N_CLASS, s // N_CLASS, 128), BF16)
    t_shape = jax.ShapeDtypeStruct((DEPTH, b, N_HEADS, HEAD_DIM, s), F32)
    s_in, s_out = _sample_specs(sample_ops, layer, lambda i, j: i * nj + j)
    in_specs = ([tok(D_MODEL), _full_spec((1, D_MODEL)), _layer_spec((D_MODEL, 7 * WIDTH), layer),
                 _full_spec((1, WIDTH)), _full_spec((1, WIDTH)),
                 _full_spec((N_GROUPS, CHUNK, CHUNK)), _full_spec((N_GROUPS, CHUNK, 1)),
                 _full_spec((1, WIDTH))] + s_in)
    args = (x,) + tuple(weights) + tuple(sample_ops)
    assert len(args) == N_W + N_S
    aliases = {}
    if kt_stack is not None:
        in_specs = in_specs + [pl.BlockSpec(memory_space=pl.ANY)] * 2
        aliases = {len(args): 2, len(args) + 1: 3}
        args = args + (kt_stack, vt_stack)
    return pl.pallas_call(
        _inproj_prompt_kernel,
        grid=(b, nj),
        in_specs=in_specs,
        out_specs=[tok(WIDTH), hp_spec, t_spec, t_spec, tok(WIDTH), s_out],
        out_shape=[jax.ShapeDtypeStruct((b, s, WIDTH), BF16), hp_shape,
                   t_shape, t_shape, jax.ShapeDtypeStruct((b, s, WIDTH), BF16),
                   jax.ShapeDtypeStruct((n_rows, ROWS, WIDTH), F32)],
        input_output_aliases=aliases,
        compiler_params=pltpu.CompilerParams(
            dimension_semantics=("parallel", "parallel"), vmem_limit_bytes=VMEM_LIMIT),
        name="inproj_prompt",
    )(*args)


def _inproj_sample(x, weights, layer):
    n = x.shape[0]
    full = lambda w: pl.BlockSpec((n, w), lambda i: (0, 0))
    sd = lambda dt: jax.ShapeDtypeStruct((n, WIDTH), dt)
    return pl.pallas_call(
        _inproj_sample_kernel,
        grid=(1,),
        in_specs=[full(D_MODEL), _full_spec((1, D_MODEL)), _layer_spec((D_MODEL, 7 * WIDTH), layer),
                  _full_spec((1, WIDTH)), _full_spec((1, WIDTH)),
                  _full_spec((N_GROUPS, N_NEW, n, 1)), _full_spec((N_GROUPS, n, 1)),
                  _full_spec((1, WIDTH))],
        out_specs=[full(WIDTH)] * 6,
        out_shape=[sd(BF16), sd(F32), sd(F32), sd(F32), sd(F32), sd(BF16)],
        compiler_params=pltpu.CompilerParams(
            dimension_semantics=("arbitrary",), vmem_limit_bytes=VMEM_LIMIT),
        name="inproj_sample",
    )(x, *weights)


def _outproj_kernel(x_ref, ya_ref, o_ref, bz_ref, gb_ref, wout_ref, fw_ref, y_ref, *, final):
    if len(o_ref.shape) == 3:
        o = jnp.concatenate([o_ref[hp] for hp in range(N_PAIRS)], axis=1).astype(F32)
    else:
        o = o_ref[...].astype(F32)
    yb = _rms(o * _silu(bz_ref[...].astype(F32)), gb_ref[...]).astype(BF16)
    ycat = jnp.concatenate([ya_ref[...], yb], axis=1)
    y = x_ref[...] + jnp.dot(ycat, wout_ref[...], preferred_element_type=F32)
    if final:
        y = _rms(y, fw_ref[...])
    y_ref[...] = y


def _outproj(x, ya, o, bz, gb, wout, fw, layer, final):
    b, s, _ = x.shape
    tm = min(OUTPROJ_TILE, s)
    tok = lambda w: pl.BlockSpec((None, tm, w), lambda i, j: (i, j, 0))
    o_spec = tok(WIDTH) if o.ndim == 3 else pl.BlockSpec((None, N_PAIRS, tm, 128), lambda i, j: (i, 0, j, 0))
    return pl.pallas_call(
        functools.partial(_outproj_kernel, final=final),
        grid=(b, s // tm),
        in_specs=[tok(D_MODEL), tok(WIDTH), o_spec,
                  tok(WIDTH), _full_spec((1, WIDTH)), _layer_spec((2 * WIDTH, D_MODEL), layer),
                  _full_spec((1, D_MODEL))],
        out_specs=tok(D_MODEL),
        out_shape=jax.ShapeDtypeStruct((b, s, D_MODEL), F32),
        compiler_params=pltpu.CompilerParams(
            dimension_semantics=("parallel", "parallel"), vmem_limit_bytes=VMEM_LIMIT),
        name="outproj",
    )(x, ya, o, bz, gb, wout, fw)


def _prompt_bucket_index():
    qi = np.arange(BLK)[:, None]
    kj = np.arange(2 * BLK)[None, :]
    part, kk = kj // BLK, kj % BLK
    d1 = 16 * (8 * (1 - part) + qi % 8 - kk % 8) + (qi // 8 - kk // 8)
    ok1 = (d1 >= 0) & (d1 <= SPAN)
    j4 = 4 * (32 * (1 - part) + qi % 32 - kk % 32) + (qi // 32 - kk // 32)
    ok4 = (j4 >= 0) & (j4 <= SPAN)
    j16 = qi - kk + 0 * part
    ok16 = (part == 1) & (j16 >= 0)
    dist = np.stack([d1, 4 * j4, 16 * j16])
    ok = np.stack([ok1, ok4, ok16])
    return np.where(ok, _t5_bucket(np.clip(dist, 0, None)), -1)


def _prompt_units(nrow):
    units = [(2, [(c, 0, BLK)], []) for c in range(N_CLASS)]
    for r in range(4):
        cls = [r + 4 * a for a in range(4)]
        for i in range(nrow // 32):
            units.append((1, [(c, 32 * i, 32) for c in cls], [(c, 32 * i - 32, 32) for c in cls] if i else []))
    allc = list(range(N_CLASS))
    for i in range(nrow // 8):
        units.append((0, [(c, 8 * i, 8) for c in allc], [(c, 8 * i - 8, 8) for c in allc] if i else []))
    return units


def _attn_prompt_kernel(*refs):
    qkv_ref, bias_ref = refs[:2]
    sample_refs = refs[2:2 + N_S]
    o_ref, os_ref, qs, ks, vs, s16_scr, s_scr, mg_scr, acc_o, acc_l, onat = refs[2 + N_S:]
    nrow = qkv_ref.shape[2]
    lane = lax.broadcasted_iota(jnp.int32, (BLK, 128), 1)
    first_head = lane < HEAD_DIM
    units = _prompt_units(nrow)
    phase = pl.program_id(2)

    def gather(ref, lead, pieces):
        return jnp.concatenate([ref[lead + (c, slice(r0, r0 + nr), slice(None))] for c, r0, nr in pieces],
                               axis=0)

    def scatter(ref, lead, pieces, val):
        off = 0
        for c, r0, nr in pieces:
            ref[lead + (c, slice(r0, r0 + nr), slice(None))] = val[off:off + nr]
            off += nr

    def score_slot(u):
        return (s16_scr, u) if u < N_CLASS else (s_scr, u - N_CLASS)

    @pl.when(phase == 0)
    def _scores():
        os_ref[...] = _attn_sample_body(*sample_refs, mxu_tiles=True)
        for c in range(N_CLASS):
            qs[c] = qkv_ref[0, c].astype(F32)
            ks[c] = qkv_ref[1, c].astype(F32)
            vs[c] = qkv_ref[2, c].astype(F32)
        for u, (cfg, q_pieces, prev_pieces) in enumerate(units):
            nk = BLK * (2 if prev_pieces else 1)
            q = gather(qs, (), q_pieces)
            q2 = jnp.concatenate([jnp.where(first_head, q, 0.0), jnp.where(first_head, 0.0, q)], axis=0)
            k = gather(ks, (), prev_pieces + q_pieces).astype(BF16)
            s = lax.dot_general(q2.astype(BF16), k, (((1,), (1,)), ((), ())), preferred_element_type=F32)
            s = s + bias_ref[cfg, :, 2 * BLK - nk:]
            ref, slot = score_slot(u)
            ref[slot, :, :nk] = s
            m = jnp.broadcast_to(jnp.max(s, axis=1, keepdims=True), (2 * BLK, 128))
            for h in range(2):
                mh = m[h * BLK:(h + 1) * BLK]
                if cfg != 2:
                    mh = jnp.maximum(mh, gather(mg_scr, (h,), q_pieces))
                scatter(mg_scr, (h,), q_pieces, mh)

    @pl.when(phase == 1)
    def _softmax():
        os_ref[...] = _attn_sample_body(*sample_refs, mxu_tiles=True)
        for u, (cfg, q_pieces, prev_pieces) in enumerate(units):
            nk = BLK * (2 if prev_pieces else 1)
            v = gather(vs, (), prev_pieces + q_pieces).astype(BF16)
            vext = jnp.concatenate([v, jnp.ones_like(v)], axis=1)
            mq = jnp.concatenate([gather(mg_scr, (0,), q_pieces), gather(mg_scr, (1,), q_pieces)], axis=0)
            if nk > BLK:
                mq = jnp.concatenate([mq, mq], axis=1)
            ref, slot = score_slot(u)
            p = jnp.exp2(ref[slot, :, :nk] - mq).astype(BF16)
            pv = jnp.dot(p, vext, preferred_element_type=F32)
            num = jnp.where(first_head, pv[:BLK, :128], pv[BLK:, :128])
            den = jnp.where(first_head, pv[:BLK, 128:], pv[BLK:, 128:])
            if cfg != 2:
                num = num + gather(acc_o, (), q_pieces)
                den = den + gather(acc_l, (), q_pieces)
            scatter(acc_o, (), q_pieces, num)
            scatter(acc_l, (), q_pieces, den)
        for c in range(N_CLASS):
            onat[pl.ds(c, nrow, stride=N_CLASS), :] = acc_o[c] / acc_l[c]
        o_ref[...] = onat[...].astype(o_ref.dtype)


def _attn_prompt(qkv_hp, bias_tab, sample_ops, layer, row0):
    b, _, _, _, nrow, _ = qkv_hp.shape
    s = nrow * N_CLASS
    n_rows = b * N_PAIRS * ATTN_PHASES
    blk = pl.BlockSpec((None, 3, None, N_CLASS, nrow, 128), lambda i, j, r: (i, 0, j, 0, 0, 0))
    out_blk = pl.BlockSpec((None, None, s, 128), lambda i, j, r: (i, j, 0, 0))
    cls = lambda lead: pltpu.VMEM(lead + (N_CLASS, nrow, 128), F32)
    n_wide = len(_prompt_units(nrow)) - N_CLASS
    s_in, s_out = _sample_specs(sample_ops, layer,
                                lambda i, j, r: row0 + (i * N_PAIRS + j) * ATTN_PHASES + r)
    s_out = pl.BlockSpec((None, ROWS, WIDTH), lambda i, j, r: ((i * N_PAIRS + j) * ATTN_PHASES + r, 0, 0))
    return pl.pallas_call(
        _attn_prompt_kernel,
        grid=(b, N_PAIRS, ATTN_PHASES),
        in_specs=[blk, pl.BlockSpec((3, None, 2 * BLK, 2 * BLK), lambda i, j, r: (0, j, 0, 0))] + s_in,
        out_specs=[out_blk, s_out],
        out_shape=[jax.ShapeDtypeStruct((b, N_PAIRS, s, 128), BF16),
                   jax.ShapeDtypeStruct((n_rows, ROWS, WIDTH), F32)],
        scratch_shapes=[cls(()), cls(()), cls(()),
                        pltpu.VMEM((N_CLASS, 2 * BLK, BLK), F32),
                        pltpu.VMEM((n_wide, 2 * BLK, 2 * BLK), F32),
                        cls((2,)), cls(()), cls(()),
                        pltpu.VMEM((s, 128), F32)],
        compiler_params=pltpu.CompilerParams(
            dimension_semantics=("parallel", "parallel", "arbitrary"), vmem_limit_bytes=VMEM_LIMIT),
        name="attn_prompt",
    )(qkv_hp, bias_tab, *sample_ops)


def _new_token_mix_selectors(nb):
    n = nb * N_NEW
    t = np.arange(n) % N_NEW
    sel_w = np.zeros((N_NEW, N_NEW, N_NEW, n), np.float32)
    sel_b = np.zeros((N_NEW, n), np.float32)
    for r in range(n):
        sel_b[t[r], r] = 1.0
        for d in range(t[r] + 1):
            sel_w[d, t[r], t[r] - d, r] = 1.0
    return jnp.asarray(sel_w), jnp.asarray(sel_b)


def kernel(x_prompt, x_sample, cache_k, cache_v, norm_w, w_in, ln_v_w, ln_v_b, w_spatial, b_spatial,
           rel_bias, out_norm_a, out_norm_b, w_out, final_norm_w):
    b, s, _ = x_prompt.shape
    nb, nt, _ = x_sample.shape
    n_s = nb * nt
    rows_inproj = b * (s // INPROJ_TILE)
    assert cache_k.shape[2] == WB and nt == N_NEW and s % (N_CLASS * BLK) == 0
    assert rows_inproj + b * N_PAIRS * ATTN_PHASES == nb

    kt_all = jnp.transpose(cache_k, (0, 1, 3, 4, 2))
    vt_all = jnp.transpose(cache_v, (0, 1, 3, 4, 2))

    bias_prompt = _bias_tables(_prompt_bucket_index(), rel_bias).reshape(3, N_PAIRS, 2 * BLK, 2 * BLK)
    sample_idx, wtab = _sample_tables()
    btab = _bias_tables(sample_idx, rel_bias)[0]
    sel, new = _sample_selectors()
    sel_w, sel_b = _new_token_mix_selectors(nb)
    hi = lax.Precision.HIGHEST

    tril = np.tril(np.ones((CHUNK, CHUNK), np.float32))
    rows8 = lambda a: jnp.concatenate([a.reshape(nb, nt, WIDTH)] * (ROWS // nt), axis=1)

    xp = x_prompt
    xs = x_sample.reshape(1, n_s, D_MODEL)
    kt_stack = vt_stack = None
    sk, sv, sc = [], [], []
    win, wout = w_in.astype(BF16), w_out.astype(BF16)
    for l in range(DEPTH):
        common = (norm_w[l][None], win, ln_v_w[l][None], ln_v_b[l][None])
        mix_p = (w_spatial[l] * tril).astype(BF16)
        mixb_p = b_spatial[l][:, :, None]
        coef_s = jnp.einsum('gts,dtsr->gdr', w_spatial[l][:, :nt, :nt], sel_w, precision=hi)[..., None]
        mixb_s = jnp.einsum('gt,tr->gr', b_spatial[l][:, :nt], sel_b, precision=hi)[..., None]
        ga, gb = out_norm_a[l][None], out_norm_b[l][None]
        fw = final_norm_w[None]
        final = l == DEPTH - 1

        ya_s, q_s, k_s, v_s, vn_s, bz_s = _inproj_sample(xs[0], common + (coef_s, mixb_s, ga), l)
        sample_ops = (jnp.stack([rows8(q_s), rows8(k_s), rows8(v_s)], axis=1), kt_all, vt_all,
                      sel, new, btab, wtab)
        ya, qkv_hp, kt_stack, vt_stack, bz, o_t0 = _inproj_prompt(
            xp, common + (mix_p, mixb_p, ga), l, kt_stack, vt_stack, sample_ops)
        o_hp, o_t1 = _attn_prompt(qkv_hp, bias_prompt, sample_ops, l, rows_inproj)
        xp = _outproj(xp, ya, o_hp, bz, gb, wout, fw, l, final)

        o_s = jnp.concatenate([o_t0, o_t1], axis=0)[:, :nt, :].reshape(1, n_s, WIDTH)
        xs = _outproj(xs, ya_s[None], o_s, bz_s[None], gb, wout, fw, l, final)
        sk.append(k_s)
        sv.append(v_s)
        sc.append(vn_s)

    heads = (N_HEADS, HEAD_DIM)
    new_k_prompt = jnp.transpose(kt_stack, (0, 1, 4, 2, 3))
    new_v_prompt = jnp.transpose(vt_stack, (0, 1, 4, 2, 3))
    new_k_sample = jnp.stack(sk).reshape((DEPTH, nb, nt) + heads)
    new_v_sample = jnp.stack(sv).reshape((DEPTH, nb, nt) + heads)
    new_vchunk = jnp.stack(sc).reshape(DEPTH, nb, nt, WIDTH)
    return (xp, xs.reshape(nb, nt, D_MODEL), new_k_prompt, new_v_prompt,
            new_k_sample, new_v_sample, new_vchunk)
```

```python
import functools

import numpy as np
import jax
import jax.numpy as jnp
from jax import lax
from jax.experimental import pallas as pl
from jax.experimental.pallas import tpu as pltpu

D_MODEL = 1024
DEPTH = 4
WIDTH = 512
N_GROUPS = 4
N_HEADS = 8
HEAD_DIM = 64
N_PAIRS = 4
CHUNK = 128
SPAN = 128
REL_BUCKETS = 32
REL_MAX_DIST = 2048
ATTN_SCALE = 0.125
LOG2E = 1.4426950408889634
Q_SCALE = ATTN_SCALE * LOG2E
EPS = 1e-6
NEG = -1e30
N_CLASS = 16
BLK = 128
INPROJ_TILE = 256
OUTPROJ_TILE = 1024
ATTN_PHASES = 2
VMEM_LIMIT = 56 * 1024 * 1024

F32 = jnp.float32
BF16 = jnp.bfloat16


def _t5_bucket(dist):
    max_exact = REL_BUCKETS // 2
    large = max_exact + (np.log(np.maximum(dist, 1).astype(np.float32) / max_exact)
                         / np.log(REL_MAX_DIST / max_exact) * (REL_BUCKETS - max_exact)).astype(np.int32)
    large = np.minimum(large, REL_BUCKETS - 1)
    return np.where(dist < max_exact, dist, large).astype(np.int32)


def _rms(x, w):
    return x * lax.rsqrt(jnp.mean(x * x, axis=-1, keepdims=True) + EPS) * w


def _silu(z):
    return z / (1.0 + jnp.exp(-z))


def _full_spec(shape):
    return pl.BlockSpec(shape, lambda *_: (0,) * len(shape))


def _layer_spec(shape, layer):
    return pl.BlockSpec((None,) + shape, lambda *_: (layer,) + (0,) * len(shape))


def _bias_table_kernel(idx_ref, rb_ref, out_ref):
    idx = idx_ref[...]
    for h in range(N_HEADS):
        acc = jnp.full(idx.shape, NEG, F32)
        for bucket in range(REL_BUCKETS):
            acc = jnp.where(idx == bucket, rb_ref[bucket, h] * LOG2E, acc)
        out_ref[h] = acc


def _bias_tables(idx, rel_bias):
    n, r, c = idx.shape
    return pl.pallas_call(
        _bias_table_kernel,
        grid=(n,),
        in_specs=[pl.BlockSpec((None, r, c), lambda i: (i, 0, 0)),
                  pl.BlockSpec(memory_space=pltpu.SMEM)],
        out_specs=pl.BlockSpec((None, N_HEADS, r, c), lambda i: (i, 0, 0, 0)),
        out_shape=jax.ShapeDtypeStruct((n, N_HEADS, r, c), F32),
        compiler_params=pltpu.CompilerParams(dimension_semantics=("parallel",)),
        name="bias_tables",
    )(jnp.asarray(idx, jnp.int32), rel_bias.astype(F32))


N_NEW = 4
ROWS = 8
WB = 2048
N_TILES = WB // 128
NPOS = WB + 128
FAR_TILES = 12
NEAR_TILES = 3
N_PAT = 2 + N_NEW
N_S = 7


def _sample_tables():
    r = np.arange(ROWS)[:, None]
    p = np.arange(NPOS)[None, :]
    i = r % N_NEW
    dist = np.where(p < WB, WB + i - p, i - (p - WB))
    ok = (dist >= 0) & ((p < WB) | (p - WB < N_NEW))
    mult = ((dist <= 128).astype(np.int32) + ((dist % 4 == 0) & (dist <= 512))
            + ((dist % 16 == 0) & (dist <= 2048))) * ok
    idx = np.where(mult > 0, _t5_bucket(np.clip(dist, 0, None)), -1)
    return idx[None], jnp.asarray(mult, F32)


def _sample_selectors():
    lane = np.arange(128)
    sel = np.zeros((ROWS, N_PAT * 128), np.float32)
    new = np.zeros((ROWS, 128), np.float32)
    for i in range(N_NEW):
        sel[i, lane[lane % 16 == i]] = 1.0
        sel[i, 128 + lane[lane % 4 == i]] = 1.0
        sel[i, (2 + i) * 128:(3 + i) * 128] = 1.0
        new[i, i] = 1.0
    return jnp.asarray(sel, BF16), jnp.asarray(new, BF16)


def _sample_specs(sample_ops, layer, row_of):
    sel, new, btab, wtab = sample_ops[3:]
    rows = pl.BlockSpec((None, 3, ROWS, WIDTH), lambda *g: (row_of(*g), 0, 0, 0))
    buf = pl.BlockSpec((None, None, N_HEADS, HEAD_DIM, WB), lambda *g: (layer, row_of(*g), 0, 0, 0))
    in_specs = [rows, buf, buf, _full_spec(sel.shape), _full_spec(new.shape),
                _full_spec(btab.shape), _full_spec(wtab.shape)]
    out_spec = pl.BlockSpec((None, ROWS, WIDTH), lambda *g: (row_of(*g), 0, 0))
    return in_specs, out_spec


def _attn_sample_body(qkv_ref, kt_ref, vt_ref, sel_ref, new_ref, b_ref, w_ref, *, mxu_tiles):
    tn = (((0,), (0,)), ((), ()))
    nt = (((1,), (1,)), ((), ()))
    q_ref, k_ref, v_ref = qkv_ref.at[0], qkv_ref.at[1], qkv_ref.at[2]
    row = lax.broadcasted_iota(jnp.int32, (1, ROWS, 128), 1)
    lane = lax.broadcasted_iota(jnp.int32, (1, 128), 1)
    shape3 = (N_HEADS, ROWS, 128)

    def selected(ref, sel, j):
        t = lax.dot_general(ref[...].astype(BF16), sel, tn, preferred_element_type=F32)
        return [t[:, i * 128:(i + 1) * 128].reshape(N_HEADS, HEAD_DIM, 128) for i in range(j)]

    def token_columns(ref):
        xt = ref[...].T
        return [jnp.broadcast_to(xt[:, i:i + 1], (WIDTH, 128)) for i in range(N_NEW)]

    def by_lane(cols, key):
        out = jnp.zeros((WIDTH, 128), F32)
        for i in range(N_NEW):
            out = jnp.where(key == i, cols[i], out)
        return out.reshape(N_HEADS, HEAD_DIM, 128)

    if mxu_tiles:
        pats = selected(q_ref, sel_ref[...], N_PAT)
        pat16, pat4, tok = pats[0], pats[1], pats[2:]
        (ktn,), (vtn,) = selected(k_ref, new_ref[...], 1), selected(v_ref, new_ref[...], 1)
    else:
        q_cols = token_columns(q_ref)
        tok = [c.reshape(N_HEADS, HEAD_DIM, 128) for c in q_cols]
        pat16, pat4 = by_lane(q_cols, lane % 16), by_lane(q_cols, lane % 4)
        ktn, vtn = by_lane(token_columns(k_ref), lane), by_lane(token_columns(v_ref), lane)
    s_tiles = []
    for j in range(FAR_TILES + NEAR_TILES):
        pat = pat16 if j < FAR_TILES else pat4
        kt = kt_ref[:, :, j * 128:(j + 1) * 128]
        s_tiles.append(jnp.broadcast_to(jnp.sum(kt * pat, axis=1, keepdims=True), shape3))
    for src in (kt_ref[:, :, WB - 128:], ktn):
        t = jnp.zeros(shape3, F32)
        for i in range(N_NEW):
            s_i = jnp.sum(src * tok[i], axis=1, keepdims=True)
            t = jnp.where(row % N_NEW == i, s_i, t)
        s_tiles.append(t)
    ntile = len(s_tiles)
    s_tiles = [s_tiles[j] + b_ref[:, :, j * 128:(j + 1) * 128] for j in range(ntile)]
    m = s_tiles[0]
    for t in s_tiles[1:]:
        m = jnp.maximum(m, t)
    m = jnp.max(m, axis=2, keepdims=True)
    p_tiles = [jnp.exp2(s_tiles[j] - m) * w_ref[:, j * 128:(j + 1) * 128][None] for j in range(ntile)]
    l = p_tiles[0]
    for t in p_tiles[1:]:
        l = l + t
    inv = 1.0 / jnp.sum(l, axis=2, keepdims=True)
    p_tiles = [(t * inv).astype(BF16) for t in p_tiles]

    heads = []
    for h in range(N_HEADS):
        p_win = jnp.concatenate([p_tiles[j][h] for j in range(N_TILES)], axis=1)
        vt_h, vn_h = vt_ref[h].astype(BF16), vtn[h].astype(BF16)
        if mxu_tiles:
            o = lax.dot_general(vt_h, p_win, nt, preferred_element_type=F32)
            o = o + lax.dot_general(vn_h, p_tiles[N_TILES][h], nt, preferred_element_type=F32)
        else:
            o = lax.dot_general(p_win, vt_h, nt, preferred_element_type=F32)
            o = o + lax.dot_general(p_tiles[N_TILES][h], vn_h, nt, preferred_element_type=F32)
        heads.append(o)
    return jnp.concatenate(heads, axis=0).T if mxu_tiles else jnp.concatenate(heads, axis=1)


def _inproj_core(x_ref, nw_ref, win_ref, lnw_ref, lnb_ref, ga_ref, mix_fn):
    h = _rms(x_ref[...], nw_ref[...]).astype(BF16)

    def proj(j):
        return jnp.dot(h, win_ref[:, j * WIDTH:(j + 1) * WIDTH], preferred_element_type=F32)

    a_v = proj(1)
    mu = jnp.mean(a_v, axis=-1, keepdims=True)
    xc = a_v - mu
    vn = xc * lax.rsqrt(jnp.mean(xc * xc, axis=-1, keepdims=True) + EPS) * lnw_ref[...] + lnb_ref[...]
    a_out = proj(0) * mix_fn(vn)
    ya = _rms(a_out * _silu(proj(2)), ga_ref[...]).astype(BF16)
    q = proj(3) * Q_SCALE
    return ya, vn, q, proj(4), proj(5), proj(6)


def _mix_chunks(vn, mix_ref, mixb_ref):
    tm = vn.shape[0]
    chunk = mix_ref.shape[1]
    nch = tm // chunk
    vnb = vn.astype(BF16)
    cols = []
    for g in range(N_GROUPS):
        vg = vnb[:, g * 128:(g + 1) * 128]
        if nch > 1:
            vg = jnp.concatenate([vg[ci * chunk:(ci + 1) * chunk] for ci in range(nch)], axis=1)
        r = jnp.dot(mix_ref[g], vg, preferred_element_type=F32) + mixb_ref[g]
        if nch > 1:
            r = jnp.concatenate([r[:, ci * 128:(ci + 1) * 128] for ci in range(nch)], axis=0)
        cols.append(r)
    return jnp.concatenate(cols, axis=1)


def _mix_new_tokens(vn, coef_ref, mixb_ref):
    cols = []
    for g in range(N_GROUPS):
        vg = vn[:, g * 128:(g + 1) * 128]
        acc = coef_ref[g, 0] * vg + mixb_ref[g]
        for d in range(1, N_NEW):
            acc = acc + coef_ref[g, d] * pltpu.roll(vg, d, axis=0)
        cols.append(acc)
    return jnp.concatenate(cols, axis=1)


N_W = 8


def _inproj_prompt_kernel(*refs):
    x_ref, nw_ref, win_ref, lnw_ref, lnb_ref, mix_ref, mixb_ref, ga_ref = refs[:N_W]
    sample_refs = refs[N_W:N_W + N_S]
    ya_ref, qkv_ref, kt_ref, vt_ref, bz_ref, os_ref = refs[-6:]
    tm = x_ref.shape[0]
    ya, _, q, k, v, bz = _inproj_core(
        x_ref, nw_ref, win_ref, lnw_ref, lnb_ref, ga_ref,
        functools.partial(_mix_chunks, mix_ref=mix_ref, mixb_ref=mixb_ref))
    ya_ref[...] = ya
    bz_ref[...] = bz.astype(bz_ref.dtype)
    for t, val in enumerate((q, k, v)):
        for hp in range(N_PAIRS):
            blk = val[:, hp * 128:(hp + 1) * 128].reshape(tm // N_CLASS, N_CLASS, 128)
            qkv_ref[t, hp] = jnp.swapaxes(blk, 0, 1).astype(BF16)
    kt_ref[...] = k.T.reshape(N_HEADS, HEAD_DIM, tm)
    vt_ref[...] = v.T.reshape(N_HEADS, HEAD_DIM, tm)
    os_ref[...] = _attn_sample_body(*sample_refs, mxu_tiles=False)


def _inproj_sample_kernel(x_ref, nw_ref, win_ref, lnw_ref, lnb_ref, coef_ref, mixb_ref, ga_ref,
                          ya_ref, q_ref, k_ref, v_ref, vn_ref, bz_ref):
    ya, vn, q, k, v, bz = _inproj_core(
        x_ref, nw_ref, win_ref, lnw_ref, lnb_ref, ga_ref,
        functools.partial(_mix_new_tokens, coef_ref=coef_ref, mixb_ref=mixb_ref))
    ya_ref[...] = ya
    q_ref[...] = q
    k_ref[...] = k
    v_ref[...] = v
    vn_ref[...] = vn
    bz_ref[...] = bz.astype(bz_ref.dtype)


def _inproj_prompt(x, weights, layer, kt_stack, vt_stack, sample_ops):
    b, s, _ = x.shape
    tm = INPROJ_TILE
    nj = s // tm
    n_rows = b * nj
    tok = lambda w: pl.BlockSpec((None, tm, w), lambda i, j: (i, j, 0))
    hp_spec = pl.BlockSpec((None, None, 3, N_PAIRS, N_CLASS, tm // N_CLASS, 128),
                           lambda i, j: (i, j, 0, 0, 0, 0, 0))
    t_spec = pl.BlockSpec((None, None, N_HEADS, HEAD_DIM, tm), lambda i, j: (layer, i, 0, 0, j))
    hp_shape = jax.ShapeDtypeStruct((b, nj, 3, N_PAIRS, N_CLASS, tm // N_CLASS, 128), BF16)
    t_shape = jax.ShapeDtypeStruct((DEPTH, b, N_HEADS, HEAD_DIM, s), F32)
    s_in, s_out = _sample_specs(sample_ops, layer, lambda i, j: i * nj + j)
    in_specs = ([tok(D_MODEL), _full_spec((1, D_MODEL)), _layer_spec((D_MODEL, 7 * WIDTH), layer),
                 _full_spec((1, WIDTH)), _full_spec((1, WIDTH)),
                 _full_spec((N_GROUPS, CHUNK, CHUNK)), _full_spec((N_GROUPS, CHUNK, 1)),
                 _full_spec((1, WIDTH))] + s_in)
    args = (x,) + tuple(weights) + tuple(sample_ops)
    assert len(args) == N_W + N_S
    aliases = {}
    if kt_stack is not None:
        in_specs = in_specs + [pl.BlockSpec(memory_space=pl.ANY)] * 2
        aliases = {len(args): 2, len(args) + 1: 3}
        args = args + (kt_stack, vt_stack)
    return pl.pallas_call(
        _inproj_prompt_kernel,
        grid=(b, nj),
        in_specs=in_specs,
        out_specs=[tok(WIDTH), hp_spec, t_spec, t_spec, tok(WIDTH), s_out],
        out_shape=[jax.ShapeDtypeStruct((b, s, WIDTH), BF16), hp_shape,
                   t_shape, t_shape, jax.ShapeDtypeStruct((b, s, WIDTH), BF16),
                   jax.ShapeDtypeStruct((n_rows, ROWS, WIDTH), F32)],
        input_output_aliases=aliases,
        compiler_params=pltpu.CompilerParams(
            dimension_semantics=("parallel", "parallel"), vmem_limit_bytes=VMEM_LIMIT),
        name="inproj_prompt",
    )(*args)


def _inproj_sample(x, weights, layer):
    n = x.shape[0]
    full = lambda w: pl.BlockSpec((n, w), lambda i: (0, 0))
    sd = lambda dt: jax.ShapeDtypeStruct((n, WIDTH), dt)
    return pl.pallas_call(
        _inproj_sample_kernel,
        grid=(1,),
        in_specs=[full(D_MODEL), _full_spec((1, D_MODEL)), _layer_spec((D_MODEL, 7 * WIDTH), layer),
                  _full_spec((1, WIDTH)), _full_spec((1, WIDTH)),
                  _full_spec((N_GROUPS, N_NEW, n, 1)), _full_spec((N_GROUPS, n, 1)),
                  _full_spec((1, WIDTH))],
        out_specs=[full(WIDTH)] * 6,
        out_shape=[sd(BF16), sd(F32), sd(F32), sd(F32), sd(F32), sd(BF16)],
        compiler_params=pltpu.CompilerParams(
            dimension_semantics=("arbitrary",), vmem_limit_bytes=VMEM_LIMIT),
        name="inproj_sample",
    )(x, *weights)


def _outproj_kernel(x_ref, ya_ref, o_ref, bz_ref, gb_ref, wout_ref, fw_ref, y_ref, *, final):
    if len(o_ref.shape) == 3:
        o = jnp.concatenate([o_ref[hp] for hp in range(N_PAIRS)], axis=1).astype(F32)
    else:
        o = o_ref[...].astype(F32)
    yb = _rms(o * _silu(bz_ref[...].astype(F32)), gb_ref[...]).astype(BF16)
    ycat = jnp.concatenate([ya_ref[...], yb], axis=1)
    y = x_ref[...] + jnp.dot(ycat, wout_ref[...], preferred_element_type=F32)
    if final:
        y = _rms(y, fw_ref[...])
    y_ref[...] = y


def _outproj(x, ya, o, bz, gb, wout, fw, layer, final):
    b, s, _ = x.shape
    tm = min(OUTPROJ_TILE, s)
    tok = lambda w: pl.BlockSpec((None, tm, w), lambda i, j: (i, j, 0))
    o_spec = tok(WIDTH) if o.ndim == 3 else pl.BlockSpec((None, N_PAIRS, tm, 128), lambda i, j: (i, 0, j, 0))
    return pl.pallas_call(
        functools.partial(_outproj_kernel, final=final),
        grid=(b, s // tm),
        in_specs=[tok(D_MODEL), tok(WIDTH), o_spec,
                  tok(WIDTH), _full_spec((1, WIDTH)), _layer_spec((2 * WIDTH, D_MODEL), layer),
                  _full_spec((1, D_MODEL))],
        out_specs=tok(D_MODEL),
        out_shape=jax.ShapeDtypeStruct((b, s, D_MODEL), F32),
        compiler_params=pltpu.CompilerParams(
            dimension_semantics=("parallel", "parallel"), vmem_limit_bytes=VMEM_LIMIT),
        name="outproj",
    )(x, ya, o, bz, gb, wout, fw)


def _prompt_bucket_index():
    qi = np.arange(BLK)[:, None]
    kj = np.arange(2 * BLK)[None, :]
    part, kk = kj // BLK, kj % BLK
    d1 = 16 * (8 * (1 - part) + qi % 8 - kk % 8) + (qi // 8 - kk // 8)
    ok1 = (d1 >= 0) & (d1 <= SPAN)
    j4 = 4 * (32 * (1 - part) + qi % 32 - kk % 32) + (qi // 32 - kk // 32)
    ok4 = (j4 >= 0) & (j4 <= SPAN)
    j16 = qi - kk + 0 * part
    ok16 = (part == 1) & (j16 >= 0)
    dist = np.stack([d1, 4 * j4, 16 * j16])
    ok = np.stack([ok1, ok4, ok16])
    return np.where(ok, _t5_bucket(np.clip(dist, 0, None)), -1)


def _prompt_units(nrow):
    units = [(2, [(c, 0, BLK)], []) for c in range(N_CLASS)]
    for r in range(4):
        cls = [r + 4 * a for a in range(4)]
        for i in range(nrow // 32):
            units.append((1, [(c, 32 * i, 32) for c in cls], [(c, 32 * i - 32, 32) for c in cls] if i else []))
    allc = list(range(N_CLASS))
    for i in range(nrow // 8):
        units.append((0, [(c, 8 * i, 8) for c in allc], [(c, 8 * i - 8, 8) for c in allc] if i else []))
    return units


def _attn_prompt_kernel(*refs):
    qkv_ref, bias_ref = refs[:2]
    sample_refs = refs[2:2 + N_S]
    o_ref, os_ref, qs, ks, vs, s16_scr, s_scr, mg_scr, acc_o, acc_l, onat = refs[2 + N_S:]
    n_tiles, _, _, tile_rows, _ = qkv_ref.shape
    nrow = n_tiles * tile_rows
    lane = lax.broadcasted_iota(jnp.int32, (BLK, 128), 1)
    first_head = lane < HEAD_DIM
    units = _prompt_units(nrow)
    phase = pl.program_id(2)

    def gather(ref, lead, pieces):
        return jnp.concatenate([ref[lead + (c, slice(r0, r0 + nr), slice(None))] for c, r0, nr in pieces],
                               axis=0)

    def scatter(ref, lead, pieces, val):
        off = 0
        for c, r0, nr in pieces:
            ref[lead + (c, slice(r0, r0 + nr), slice(None))] = val[off:off + nr]
            off += nr

    def score_slot(u):
        return (s16_scr, u) if u < N_CLASS else (s_scr, u - N_CLASS)

    @pl.when(phase == 0)
    def _scores():
        os_ref[...] = _attn_sample_body(*sample_refs, mxu_tiles=True)
        for c in range(N_CLASS):
            for j in range(n_tiles):
                rows = slice(j * tile_rows, (j + 1) * tile_rows)
                qs[c, rows] = qkv_ref[j, 0, c].astype(F32)
                ks[c, rows] = qkv_ref[j, 1, c].astype(F32)
                vs[c, rows] = qkv_ref[j, 2, c].astype(F32)
        for u, (cfg, q_pieces, prev_pieces) in enumerate(units):
            nk = BLK * (2 if prev_pieces else 1)
            q = gather(qs, (), q_pieces)
            q2 = jnp.concatenate([jnp.where(first_head, q, 0.0), jnp.where(first_head, 0.0, q)], axis=0)
            k = gather(ks, (), prev_pieces + q_pieces).astype(BF16)
            s = lax.dot_general(q2.astype(BF16), k, (((1,), (1,)), ((), ())), preferred_element_type=F32)
            s = s + bias_ref[cfg, :, 2 * BLK - nk:]
            ref, slot = score_slot(u)
            ref[slot, :, :nk] = s
            m = jnp.broadcast_to(jnp.max(s, axis=1, keepdims=True), (2 * BLK, 128))
            for h in range(2):
                mh = m[h * BLK:(h + 1) * BLK]
                if cfg != 2:
                    mh = jnp.maximum(mh, gather(mg_scr, (h,), q_pieces))
                scatter(mg_scr, (h,), q_pieces, mh)

    @pl.when(phase == 1)
    def _softmax():
        os_ref[...] = _attn_sample_body(*sample_refs, mxu_tiles=True)
        for u, (cfg, q_pieces, prev_pieces) in enumerate(units):
            nk = BLK * (2 if prev_pieces else 1)
            v = gather(vs, (), prev_pieces + q_pieces).astype(BF16)
            vext = jnp.concatenate([v, jnp.ones_like(v)], axis=1)
            mq = jnp.concatenate([gather(mg_scr, (0,), q_pieces), gather(mg_scr, (1,), q_pieces)], axis=0)
            if nk > BLK:
                mq = jnp.concatenate([mq, mq], axis=1)
            ref, slot = score_slot(u)
            p = jnp.exp2(ref[slot, :, :nk] - mq).astype(BF16)
            pv = jnp.dot(p, vext, preferred_element_type=F32)
            num = jnp.where(first_head, pv[:BLK, :128], pv[BLK:, :128])
            den = jnp.where(first_head, pv[:BLK, 128:], pv[BLK:, 128:])
            if cfg != 2:
                num = num + gather(acc_o, (), q_pieces)
                den = den + gather(acc_l, (), q_pieces)
            scatter(acc_o, (), q_pieces, num)
            scatter(acc_l, (), q_pieces, den)
        for c in range(N_CLASS):
            onat[pl.ds(c, nrow, stride=N_CLASS), :] = acc_o[c] / acc_l[c]
        o_ref[...] = onat[...].astype(o_ref.dtype)


def _attn_prompt(qkv_hp, bias_tab, sample_ops, layer, row0):
    b, n_tiles, _, _, _, tile_rows, _ = qkv_hp.shape
    nrow = n_tiles * tile_rows
    s = nrow * N_CLASS
    n_rows = b * N_PAIRS * ATTN_PHASES
    blk = pl.BlockSpec((None, n_tiles, 3, None, N_CLASS, tile_rows, 128),
                       lambda i, j, r: (i, 0, 0, j, 0, 0, 0))
    out_blk = pl.BlockSpec((None, None, s, 128), lambda i, j, r: (i, j, 0, 0))
    cls = lambda lead: pltpu.VMEM(lead + (N_CLASS, nrow, 128), F32)
    n_wide = len(_prompt_units(nrow)) - N_CLASS
    s_in, s_out = _sample_specs(sample_ops, layer,
                                lambda i, j, r: row0 + (i * N_PAIRS + j) * ATTN_PHASES + r)
    s_out = pl.BlockSpec((None, ROWS, WIDTH), lambda i, j, r: ((i * N_PAIRS + j) * ATTN_PHASES + r, 0, 0))
    return pl.pallas_call(
        _attn_prompt_kernel,
        grid=(b, N_PAIRS, ATTN_PHASES),
        in_specs=[blk, pl.BlockSpec((3, None, 2 * BLK, 2 * BLK), lambda i, j, r: (0, j, 0, 0))] + s_in,
        out_specs=[out_blk, s_out],
        out_shape=[jax.ShapeDtypeStruct((b, N_PAIRS, s, 128), BF16),
                   jax.ShapeDtypeStruct((n_rows, ROWS, WIDTH), F32)],
        scratch_shapes=[cls(()), cls(()), cls(()),
                        pltpu.VMEM((N_CLASS, 2 * BLK, BLK), F32),
                        pltpu.VMEM((n_wide, 2 * BLK, 2 * BLK), F32),
                        cls((2,)), cls(()), cls(()),
                        pltpu.VMEM((s, 128), F32)],
        compiler_params=pltpu.CompilerParams(
            dimension_semantics=("parallel", "parallel", "arbitrary"), vmem_limit_bytes=VMEM_LIMIT),
        name="attn_prompt",
    )(qkv_hp, bias_tab, *sample_ops)


def _new_token_mix_selectors(nb):
    n = nb * N_NEW
    t = np.arange(n) % N_NEW
    sel_w = np.zeros((N_NEW, N_NEW, N_NEW, n), np.float32)
    sel_b = np.zeros((N_NEW, n), np.float32)
    for r in range(n):
        sel_b[t[r], r] = 1.0
        for d in range(t[r] + 1):
            sel_w[d, t[r], t[r] - d, r] = 1.0
    return jnp.asarray(sel_w), jnp.asarray(sel_b)


def kernel(x_prompt, x_sample, cache_k, cache_v, norm_w, w_in, ln_v_w, ln_v_b, w_spatial, b_spatial,
           rel_bias, out_norm_a, out_norm_b, w_out, final_norm_w):
    b, s, _ = x_prompt.shape
    nb, nt, _ = x_sample.shape
    n_s = nb * nt
    rows_inproj = b * (s // INPROJ_TILE)
    assert cache_k.shape[2] == WB and nt == N_NEW and s % (N_CLASS * BLK) == 0
    assert rows_inproj + b * N_PAIRS * ATTN_PHASES == nb

    kt_all = jnp.transpose(cache_k, (0, 1, 3, 4, 2))
    vt_all = jnp.transpose(cache_v, (0, 1, 3, 4, 2))

    bias_prompt = _bias_tables(_prompt_bucket_index(), rel_bias).reshape(3, N_PAIRS, 2 * BLK, 2 * BLK)
    sample_idx, wtab = _sample_tables()
    btab = _bias_tables(sample_idx, rel_bias)[0]
    sel, new = _sample_selectors()
    sel_w, sel_b = _new_token_mix_selectors(nb)
    hi = lax.Precision.HIGHEST

    tril = np.tril(np.ones((CHUNK, CHUNK), np.float32))
    rows8 = lambda a: jnp.concatenate([a.reshape(nb, nt, WIDTH)] * (ROWS // nt), axis=1)

    xp = x_prompt
    xs = x_sample.reshape(1, n_s, D_MODEL)
    kt_stack = vt_stack = None
    sk, sv, sc = [], [], []
    win, wout = w_in.astype(BF16), w_out.astype(BF16)
    for l in range(DEPTH):
        common = (norm_w[l][None], win, ln_v_w[l][None], ln_v_b[l][None])
        mix_p = (w_spatial[l] * tril).astype(BF16)
        mixb_p = b_spatial[l][:, :, None]
        coef_s = jnp.einsum('gts,dtsr->gdr', w_spatial[l][:, :nt, :nt], sel_w, precision=hi)[..., None]
        mixb_s = jnp.einsum('gt,tr->gr', b_spatial[l][:, :nt], sel_b, precision=hi)[..., None]
        ga, gb = out_norm_a[l][None], out_norm_b[l][None]
        fw = final_norm_w[None]
        final = l == DEPTH - 1

        ya_s, q_s, k_s, v_s, vn_s, bz_s = _inproj_sample(xs[0], common + (coef_s, mixb_s, ga), l)
        sample_ops = (jnp.stack([rows8(q_s), rows8(k_s), rows8(v_s)], axis=1), kt_all, vt_all,
                      sel, new, btab, wtab)
        ya, qkv_hp, kt_stack, vt_stack, bz, o_t0 = _inproj_prompt(
            xp, common + (mix_p, mixb_p, ga), l, kt_stack, vt_stack, sample_ops)
        o_hp, o_t1 = _attn_prompt(qkv_hp, bias_prompt, sample_ops, l, rows_inproj)
        xp = _outproj(xp, ya, o_hp, bz, gb, wout, fw, l, final)

        o_s = jnp.concatenate([o_t0, o_t1], axis=0)[:, :nt, :].reshape(1, n_s, WIDTH)
        xs = _outproj(xs, ya_s[None], o_s, bz_s[None], gb, wout, fw, l, final)
        sk.append(k_s)
        sv.append(v_s)
        sc.append(vn_s)

    heads = (N_HEADS, HEAD_DIM)
    new_k_prompt = jnp.transpose(kt_stack, (0, 1, 4, 2, 3))
    new_v_prompt = jnp.transpose(vt_stack, (0, 1, 4, 2, 3))
    new_k_sample = jnp.stack(sk).reshape((DEPTH, nb, nt) + heads)
    new_v_sample = jnp.stack(sv).reshape((DEPTH, nb, nt) + heads)
    new_vchunk = jnp.stack(sc).reshape(DEPTH, nb, nt, WIDTH)
    return (xp, xs.reshape(nb, nt, D_MODEL), new_k_prompt, new_v_prompt,
            new_k_sample, new_v_sample, new_vchunk)
```

```python
import functools

import numpy as np
import jax
import jax.numpy as jnp
from jax import lax
from jax.experimental import pallas as pl
from jax.experimental.pallas import tpu as pltpu

D_MODEL = 1024
DEPTH = 4
WIDTH = 512
N_GROUPS = 4
N_HEADS = 8
HEAD_DIM = 64
N_PAIRS = 4
CHUNK = 128
SPAN = 128
REL_BUCKETS = 32
REL_MAX_DIST = 2048
ATTN_SCALE = 0.125
LOG2E = 1.4426950408889634
Q_SCALE = ATTN_SCALE * LOG2E
EPS = 1e-6
NEG = -1e30
N_CLASS = 16
BLK = 128
INPROJ_TILE = 256
OUTPROJ_TILE = 1024
ATTN_PHASES = 2
VMEM_LIMIT = 56 * 1024 * 1024

F32 = jnp.float32
BF16 = jnp.bfloat16


def _t5_bucket(dist):
    max_exact = REL_BUCKETS // 2
    large = max_exact + (np.log(np.maximum(dist, 1).astype(np.float32) / max_exact)
                         / np.log(REL_MAX_DIST / max_exact) * (REL_BUCKETS - max_exact)).astype(np.int32)
    large = np.minimum(large, REL_BUCKETS - 1)
    return np.where(dist < max_exact, dist, large).astype(np.int32)


def _rms(x, w):
    return x * lax.rsqrt(jnp.mean(x * x, axis=-1, keepdims=True) + EPS) * w


def _silu(z):
    return z / (1.0 + jnp.exp(-z))


def _full_spec(shape):
    return pl.BlockSpec(shape, lambda *_: (0,) * len(shape))


def _layer_spec(shape, layer):
    return pl.BlockSpec((None,) + shape, lambda *_: (layer,) + (0,) * len(shape))


def _bias_table_kernel(idx_ref, rb_ref, out_ref):
    idx = idx_ref[...]
    for h in range(N_HEADS):
        acc = jnp.full(idx.shape, NEG, F32)
        for bucket in range(REL_BUCKETS):
            acc = jnp.where(idx == bucket, rb_ref[bucket, h] * LOG2E, acc)
        out_ref[h] = acc


def _bias_tables(idx, rel_bias):
    n, r, c = idx.shape
    return pl.pallas_call(
        _bias_table_kernel,
        grid=(n,),
        in_specs=[pl.BlockSpec((None, r, c), lambda i: (i, 0, 0)),
                  pl.BlockSpec(memory_space=pltpu.SMEM)],
        out_specs=pl.BlockSpec((None, N_HEADS, r, c), lambda i: (i, 0, 0, 0)),
        out_shape=jax.ShapeDtypeStruct((n, N_HEADS, r, c), F32),
        compiler_params=pltpu.CompilerParams(dimension_semantics=("parallel",)),
        name="bias_tables",
    )(jnp.asarray(idx, jnp.int32), rel_bias.astype(F32))


N_NEW = 4
ROWS = 8
WB = 2048
N_TILES = WB // 128
NPOS = WB + 128
FAR_TILES = 12
NEAR_TILES = 3
N_PAT = 2 + N_NEW
N_S = 9


def _sample_tables():
    r = np.arange(ROWS)[:, None]
    p = np.arange(NPOS)[None, :]
    i = r % N_NEW
    dist = np.where(p < WB, WB + i - p, i - (p - WB))
    ok = (dist >= 0) & ((p < WB) | (p - WB < N_NEW))
    mult = ((dist <= 128).astype(np.int32) + ((dist % 4 == 0) & (dist <= 512))
            + ((dist % 16 == 0) & (dist <= 2048))) * ok
    idx = np.where(mult > 0, _t5_bucket(np.clip(dist, 0, None)), -1)
    return idx[None], jnp.asarray(mult, F32)


def _sample_selectors():
    lane = np.arange(128)
    sel = np.zeros((ROWS, N_PAT * 128), np.float32)
    new = np.zeros((ROWS, 128), np.float32)
    for i in range(N_NEW):
        sel[i, lane[lane % 16 == i]] = 1.0
        sel[i, 128 + lane[lane % 4 == i]] = 1.0
        sel[i, (2 + i) * 128:(3 + i) * 128] = 1.0
        new[i, i] = 1.0
    return jnp.asarray(sel, BF16), jnp.asarray(new, BF16)


def _sample_specs(sample_ops, layer, row_of):
    sel, new, btab, wtab = sample_ops[5:]
    rows = pl.BlockSpec((None, ROWS, WIDTH), lambda *g: (row_of(*g), 0, 0))
    buf = pl.BlockSpec((None, None, N_HEADS, HEAD_DIM, WB), lambda *g: (layer, row_of(*g), 0, 0, 0))
    in_specs = [rows, rows, rows, buf, buf, _full_spec(sel.shape), _full_spec(new.shape),
                _full_spec(btab.shape), _full_spec(wtab.shape)]
    out_spec = pl.BlockSpec((None, ROWS, WIDTH), lambda *g: (row_of(*g), 0, 0))
    return in_specs, out_spec


def _attn_sample_body(q_ref, k_ref, v_ref, kt_ref, vt_ref, sel_ref, new_ref, b_ref, w_ref, *, mxu_tiles):
    tn = (((0,), (0,)), ((), ()))
    nt = (((1,), (1,)), ((), ()))
    row = lax.broadcasted_iota(jnp.int32, (1, ROWS, 128), 1)
    lane = lax.broadcasted_iota(jnp.int32, (1, 128), 1)
    shape3 = (N_HEADS, ROWS, 128)

    def selected(ref, sel, j):
        t = lax.dot_general(ref[...].astype(BF16), sel, tn, preferred_element_type=F32)
        return [t[:, i * 128:(i + 1) * 128].reshape(N_HEADS, HEAD_DIM, 128) for i in range(j)]

    def token_columns(ref):
        xt = ref[...].T
        return [jnp.broadcast_to(xt[:, i:i + 1], (WIDTH, 128)) for i in range(N_NEW)]

    def by_lane(cols, key):
        out = jnp.zeros((WIDTH, 128), F32)
        for i in range(N_NEW):
            out = jnp.where(key == i, cols[i], out)
        return out.reshape(N_HEADS, HEAD_DIM, 128)

    if mxu_tiles:
        pats = selected(q_ref, sel_ref[...], N_PAT)
        pat16, pat4, tok = pats[0], pats[1], pats[2:]
        (ktn,), (vtn,) = selected(k_ref, new_ref[...], 1), selected(v_ref, new_ref[...], 1)
    else:
        q_cols = token_columns(q_ref)
        tok = [c.reshape(N_HEADS, HEAD_DIM, 128) for c in q_cols]
        pat16, pat4 = by_lane(q_cols, lane % 16), by_lane(q_cols, lane % 4)
        ktn, vtn = by_lane(token_columns(k_ref), lane), by_lane(token_columns(v_ref), lane)
    s_tiles = []
    for j in range(FAR_TILES + NEAR_TILES):
        pat = pat16 if j < FAR_TILES else pat4
        kt = kt_ref[:, :, j * 128:(j + 1) * 128]
        s_tiles.append(jnp.broadcast_to(jnp.sum(kt * pat, axis=1, keepdims=True), shape3))
    for src in (kt_ref[:, :, WB - 128:], ktn):
        t = jnp.zeros(shape3, F32)
        for i in range(N_NEW):
            s_i = jnp.sum(src * tok[i], axis=1, keepdims=True)
            t = jnp.where(row % N_NEW == i, s_i, t)
        s_tiles.append(t)
    ntile = len(s_tiles)
    s_tiles = [s_tiles[j] + b_ref[:, :, j * 128:(j + 1) * 128] for j in range(ntile)]
    m = s_tiles[0]
    for t in s_tiles[1:]:
        m = jnp.maximum(m, t)
    m = jnp.max(m, axis=2, keepdims=True)
    p_tiles = [jnp.exp2(s_tiles[j] - m) * w_ref[:, j * 128:(j + 1) * 128][None] for j in range(ntile)]
    l = p_tiles[0]
    for t in p_tiles[1:]:
        l = l + t
    inv = 1.0 / jnp.sum(l, axis=2, keepdims=True)
    p_tiles = [(t * inv).astype(BF16) for t in p_tiles]

    heads = []
    for h in range(N_HEADS):
        p_win = jnp.concatenate([p_tiles[j][h] for j in range(N_TILES)], axis=1)
        vt_h, vn_h = vt_ref[h].astype(BF16), vtn[h].astype(BF16)
        if mxu_tiles:
            o = lax.dot_general(vt_h, p_win, nt, preferred_element_type=F32)
            o = o + lax.dot_general(vn_h, p_tiles[N_TILES][h], nt, preferred_element_type=F32)
        else:
            o = lax.dot_general(p_win, vt_h, nt, preferred_element_type=F32)
            o = o + lax.dot_general(p_tiles[N_TILES][h], vn_h, nt, preferred_element_type=F32)
        heads.append(o)
    return jnp.concatenate(heads, axis=0).T if mxu_tiles else jnp.concatenate(heads, axis=1)


def _inproj_core(x_ref, nw_ref, win_ref, lnw_ref, lnb_ref, ga_ref, mix_fn):
    h = _rms(x_ref[...], nw_ref[...]).astype(BF16)

    def proj(j):
        return jnp.dot(h, win_ref[:, j * WIDTH:(j + 1) * WIDTH], preferred_element_type=F32)

    a_v = proj(1)
    mu = jnp.mean(a_v, axis=-1, keepdims=True)
    xc = a_v - mu
    vn = xc * lax.rsqrt(jnp.mean(xc * xc, axis=-1, keepdims=True) + EPS) * lnw_ref[...] + lnb_ref[...]
    a_out = proj(0) * mix_fn(vn)
    ya = _rms(a_out * _silu(proj(2)), ga_ref[...]).astype(BF16)
    q = proj(3) * Q_SCALE
    return ya, vn, q, proj(4), proj(5), proj(6)


def _mix_chunks(vn, mix_ref, mixb_ref):
    tm = vn.shape[0]
    chunk = mix_ref.shape[1]
    nch = tm // chunk
    vnb = vn.astype(BF16)
    cols = []
    for g in range(N_GROUPS):
        vg = vnb[:, g * 128:(g + 1) * 128]
        if nch > 1:
            vg = jnp.concatenate([vg[ci * chunk:(ci + 1) * chunk] for ci in range(nch)], axis=1)
        r = jnp.dot(mix_ref[g], vg, preferred_element_type=F32) + mixb_ref[g]
        if nch > 1:
            r = jnp.concatenate([r[:, ci * 128:(ci + 1) * 128] for ci in range(nch)], axis=0)
        cols.append(r)
    return jnp.concatenate(cols, axis=1)


def _mix_new_tokens(vn, coef_ref, mixb_ref):
    cols = []
    for g in range(N_GROUPS):
        vg = vn[:, g * 128:(g + 1) * 128]
        acc = coef_ref[g, 0] * vg + mixb_ref[g]
        for d in range(1, N_NEW):
            acc = acc + coef_ref[g, d] * pltpu.roll(vg, d, axis=0)
        cols.append(acc)
    return jnp.concatenate(cols, axis=1)


N_W = 8


def _inproj_prompt_kernel(*refs):
    x_ref, nw_ref, win_ref, lnw_ref, lnb_ref, mix_ref, mixb_ref, ga_ref = refs[:N_W]
    sample_refs = refs[N_W:N_W + N_S]
    ya_ref, q_ref, k_ref, v_ref, kt_ref, vt_ref, bz_ref, os_ref = refs[-8:]
    tm = x_ref.shape[0]
    ya, _, q, k, v, bz = _inproj_core(
        x_ref, nw_ref, win_ref, lnw_ref, lnb_ref, ga_ref,
        functools.partial(_mix_chunks, mix_ref=mix_ref, mixb_ref=mixb_ref))
    ya_ref[...] = ya
    bz_ref[...] = bz.astype(bz_ref.dtype)
    for val, ref in ((q, q_ref), (k, k_ref), (v, v_ref)):
        for hp in range(N_PAIRS):
            blk = val[:, hp * 128:(hp + 1) * 128].reshape(tm // N_CLASS, N_CLASS, 128)
            ref[hp] = jnp.swapaxes(blk, 0, 1).astype(BF16)
    kt_ref[...] = k.T.reshape(N_HEADS, HEAD_DIM, tm)
    vt_ref[...] = v.T.reshape(N_HEADS, HEAD_DIM, tm)
    os_ref[...] = _attn_sample_body(*sample_refs, mxu_tiles=False)


def _inproj_sample_kernel(x_ref, nw_ref, win_ref, lnw_ref, lnb_ref, coef_ref, mixb_ref, ga_ref,
                          ya_ref, q_ref, k_ref, v_ref, vn_ref, bz_ref):
    ya, vn, q, k, v, bz = _inproj_core(
        x_ref, nw_ref, win_ref, lnw_ref, lnb_ref, ga_ref,
        functools.partial(_mix_new_tokens, coef_ref=coef_ref, mixb_ref=mixb_ref))
    ya_ref[...] = ya
    q_ref[...] = q
    k_ref[...] = k
    v_ref[...] = v
    vn_ref[...] = vn
    bz_ref[...] = bz.astype(bz_ref.dtype)


def _inproj_prompt(x, weights, layer, kt_stack, vt_stack, sample_ops):
    b, s, _ = x.shape
    tm = INPROJ_TILE
    nj = s // tm
    n_rows = b * nj
    tok = lambda w: pl.BlockSpec((None, tm, w), lambda i, j: (i, j, 0))
    hp_spec = pl.BlockSpec((None, None, N_PAIRS, N_CLASS, tm // N_CLASS, 128),
                           lambda i, j: (i, j, 0, 0, 0, 0))
    t_spec = pl.BlockSpec((None, None, N_HEADS, HEAD_DIM, tm), lambda i, j: (layer, i, 0, 0, j))
    hp_shape = jax.ShapeDtypeStruct((b, nj, N_PAIRS, N_CLASS, tm // N_CLASS, 128), BF16)
    t_shape = jax.ShapeDtypeStruct((DEPTH, b, N_HEADS, HEAD_DIM, s), F32)
    s_in, s_out = _sample_specs(sample_ops, layer, lambda i, j: i * nj + j)
    in_specs = ([tok(D_MODEL), _full_spec((1, D_MODEL)), _layer_spec((D_MODEL, 7 * WIDTH), layer),
                 _full_spec((1, WIDTH)), _full_spec((1, WIDTH)),
                 _full_spec((N_GROUPS, CHUNK, CHUNK)), _full_spec((N_GROUPS, CHUNK, 1)),
                 _full_spec((1, WIDTH))] + s_in)
    args = (x,) + tuple(weights) + tuple(sample_ops)
    assert len(args) == N_W + N_S
    aliases = {}
    if kt_stack is not None:
        in_specs = in_specs + [pl.BlockSpec(memory_space=pl.ANY)] * 2
        aliases = {len(args): 4, len(args) + 1: 5}
        args = args + (kt_stack, vt_stack)
    return pl.pallas_call(
        _inproj_prompt_kernel,
        grid=(b, nj),
        in_specs=in_specs,
        out_specs=[tok(WIDTH), hp_spec, hp_spec, hp_spec, t_spec, t_spec, tok(WIDTH), s_out],
        out_shape=[jax.ShapeDtypeStruct((b, s, WIDTH), BF16), hp_shape, hp_shape, hp_shape,
                   t_shape, t_shape, jax.ShapeDtypeStruct((b, s, WIDTH), BF16),
                   jax.ShapeDtypeStruct((n_rows, ROWS, WIDTH), F32)],
        input_output_aliases=aliases,
        compiler_params=pltpu.CompilerParams(
            dimension_semantics=("parallel", "parallel"), vmem_limit_bytes=VMEM_LIMIT),
        name="inproj_prompt",
    )(*args)


def _inproj_sample(x, weights, layer):
    n = x.shape[0]
    full = lambda w: pl.BlockSpec((n, w), lambda i: (0, 0))
    sd = lambda dt: jax.ShapeDtypeStruct((n, WIDTH), dt)
    return pl.pallas_call(
        _inproj_sample_kernel,
        grid=(1,),
        in_specs=[full(D_MODEL), _full_spec((1, D_MODEL)), _layer_spec((D_MODEL, 7 * WIDTH), layer),
                  _full_spec((1, WIDTH)), _full_spec((1, WIDTH)),
                  _full_spec((N_GROUPS, N_NEW, n, 1)), _full_spec((N_GROUPS, n, 1)),
                  _full_spec((1, WIDTH))],
        out_specs=[full(WIDTH)] * 6,
        out_shape=[sd(BF16), sd(F32), sd(F32), sd(F32), sd(F32), sd(BF16)],
        compiler_params=pltpu.CompilerParams(
            dimension_semantics=("arbitrary",), vmem_limit_bytes=VMEM_LIMIT),
        name="inproj_sample",
    )(x, *weights)


def _outproj_kernel(x_ref, ya_ref, o_ref, bz_ref, gb_ref, wout_ref, fw_ref, y_ref, *, final):
    if len(o_ref.shape) == 3:
        o = jnp.concatenate([o_ref[hp] for hp in range(N_PAIRS)], axis=1).astype(F32)
    else:
        o = o_ref[...].astype(F32)
    yb = _rms(o * _silu(bz_ref[...].astype(F32)), gb_ref[...]).astype(BF16)
    ycat = jnp.concatenate([ya_ref[...], yb], axis=1)
    y = x_ref[...] + jnp.dot(ycat, wout_ref[...], preferred_element_type=F32)
    if final:
        y = _rms(y, fw_ref[...])
    y_ref[...] = y


def _outproj(x, ya, o, bz, gb, wout, fw, layer, final):
    b, s, _ = x.shape
    tm = min(OUTPROJ_TILE, s)
    tok = lambda w: pl.BlockSpec((None, tm, w), lambda i, j: (i, j, 0))
    o_spec = tok(WIDTH) if o.ndim == 3 else pl.BlockSpec((None, N_PAIRS, tm, 128), lambda i, j: (i, 0, j, 0))
    return pl.pallas_call(
        functools.partial(_outproj_kernel, final=final),
        grid=(b, s // tm),
        in_specs=[tok(D_MODEL), tok(WIDTH), o_spec,
                  tok(WIDTH), _full_spec((1, WIDTH)), _layer_spec((2 * WIDTH, D_MODEL), layer),
                  _full_spec((1, D_MODEL))],
        out_specs=tok(D_MODEL),
        out_shape=jax.ShapeDtypeStruct((b, s, D_MODEL), F32),
        compiler_params=pltpu.CompilerParams(
            dimension_semantics=("parallel", "parallel"), vmem_limit_bytes=VMEM_LIMIT),
        name="outproj",
    )(x, ya, o, bz, gb, wout, fw)


def _prompt_bucket_index():
    qi = np.arange(BLK)[:, None]
    kj = np.arange(2 * BLK)[None, :]
    part, kk = kj // BLK, kj % BLK
    d1 = 16 * (8 * (1 - part) + qi % 8 - kk % 8) + (qi // 8 - kk // 8)
    ok1 = (d1 >= 0) & (d1 <= SPAN)
    j4 = 4 * (32 * (1 - part) + qi % 32 - kk % 32) + (qi // 32 - kk // 32)
    ok4 = (j4 >= 0) & (j4 <= SPAN)
    j16 = qi - kk + 0 * part
    ok16 = (part == 1) & (j16 >= 0)
    dist = np.stack([d1, 4 * j4, 16 * j16])
    ok = np.stack([ok1, ok4, ok16])
    return np.where(ok, _t5_bucket(np.clip(dist, 0, None)), -1)


def _prompt_units(nrow):
    units = [(2, [(c, 0, BLK)], []) for c in range(N_CLASS)]
    for r in range(4):
        cls = [r + 4 * a for a in range(4)]
        for i in range(nrow // 32):
            units.append((1, [(c, 32 * i, 32) for c in cls], [(c, 32 * i - 32, 32) for c in cls] if i else []))
    allc = list(range(N_CLASS))
    for i in range(nrow // 8):
        units.append((0, [(c, 8 * i, 8) for c in allc], [(c, 8 * i - 8, 8) for c in allc] if i else []))
    return units


def _attn_prompt_kernel(*refs):
    q_ref, k_ref, v_ref, bias_ref = refs[:4]
    sample_refs = refs[4:4 + N_S]
    o_ref, os_ref, qs, ks, vs, s16_scr, s_scr, mg_scr, acc_o, acc_l, onat = refs[4 + N_S:]
    n_tiles, _, tile_rows, _ = q_ref.shape
    nrow = n_tiles * tile_rows
    lane = lax.broadcasted_iota(jnp.int32, (BLK, 128), 1)
    first_head = lane < HEAD_DIM
    units = _prompt_units(nrow)
    phase = pl.program_id(2)

    def gather(ref, lead, pieces):
        return jnp.concatenate([ref[lead + (c, slice(r0, r0 + nr), slice(None))] for c, r0, nr in pieces],
                               axis=0)

    def scatter(ref, lead, pieces, val):
        off = 0
        for c, r0, nr in pieces:
            ref[lead + (c, slice(r0, r0 + nr), slice(None))] = val[off:off + nr]
            off += nr

    def score_slot(u):
        return (s16_scr, u) if u < N_CLASS else (s_scr, u - N_CLASS)

    @pl.when(phase == 0)
    def _scores():
        os_ref[...] = _attn_sample_body(*sample_refs, mxu_tiles=True)
        for c in range(N_CLASS):
            for j in range(n_tiles):
                rows = slice(j * tile_rows, (j + 1) * tile_rows)
                qs[c, rows] = q_ref[j, c].astype(F32)
                ks[c, rows] = k_ref[j, c].astype(F32)
                vs[c, rows] = v_ref[j, c].astype(F32)
        for u, (cfg, q_pieces, prev_pieces) in enumerate(units):
            nk = BLK * (2 if prev_pieces else 1)
            q = gather(qs, (), q_pieces)
            q2 = jnp.concatenate([jnp.where(first_head, q, 0.0), jnp.where(first_head, 0.0, q)], axis=0)
            k = gather(ks, (), prev_pieces + q_pieces).astype(BF16)
            s = lax.dot_general(q2.astype(BF16), k, (((1,), (1,)), ((), ())), preferred_element_type=F32)
            s = s + bias_ref[cfg, :, 2 * BLK - nk:]
            ref, slot = score_slot(u)
            ref[slot, :, :nk] = s
            m = jnp.broadcast_to(jnp.max(s, axis=1, keepdims=True), (2 * BLK, 128))
            for h in range(2):
                mh = m[h * BLK:(h + 1) * BLK]
                if cfg != 2:
                    mh = jnp.maximum(mh, gather(mg_scr, (h,), q_pieces))
                scatter(mg_scr, (h,), q_pieces, mh)

    @pl.when(phase == 1)
    def _softmax():
        os_ref[...] = _attn_sample_body(*sample_refs, mxu_tiles=True)
        for u, (cfg, q_pieces, prev_pieces) in enumerate(units):
            nk = BLK * (2 if prev_pieces else 1)
            v = gather(vs, (), prev_pieces + q_pieces).astype(BF16)
            vext = jnp.concatenate([v, jnp.ones_like(v)], axis=1)
            mq = jnp.concatenate([gather(mg_scr, (0,), q_pieces), gather(mg_scr, (1,), q_pieces)], axis=0)
            if nk > BLK:
                mq = jnp.concatenate([mq, mq], axis=1)
            ref, slot = score_slot(u)
            p = jnp.exp2(ref[slot, :, :nk] - mq).astype(BF16)
            pv = jnp.dot(p, vext, preferred_element_type=F32)
            num = jnp.where(first_head, pv[:BLK, :128], pv[BLK:, :128])
            den = jnp.where(first_head, pv[:BLK, 128:], pv[BLK:, 128:])
            if cfg != 2:
                num = num + gather(acc_o, (), q_pieces)
                den = den + gather(acc_l, (), q_pieces)
            scatter(acc_o, (), q_pieces, num)
            scatter(acc_l, (), q_pieces, den)
        for c in range(N_CLASS):
            onat[pl.ds(c, nrow, stride=N_CLASS), :] = acc_o[c] / acc_l[c]
        o_ref[...] = onat[...].astype(o_ref.dtype)


def _attn_prompt(q_hp, k_hp, v_hp, bias_tab, sample_ops, layer, row0):
    b, n_tiles, _, _, tile_rows, _ = q_hp.shape
    nrow = n_tiles * tile_rows
    s = nrow * N_CLASS
    n_rows = b * N_PAIRS * ATTN_PHASES
    blk = pl.BlockSpec((None, n_tiles, None, N_CLASS, tile_rows, 128), lambda i, j, r: (i, 0, j, 0, 0, 0))
    out_blk = pl.BlockSpec((None, None, s, 128), lambda i, j, r: (i, j, 0, 0))
    cls = lambda lead: pltpu.VMEM(lead + (N_CLASS, nrow, 128), F32)
    n_wide = len(_prompt_units(nrow)) - N_CLASS
    s_in, s_out = _sample_specs(sample_ops, layer,
                                lambda i, j, r: row0 + (i * N_PAIRS + j) * ATTN_PHASES + r)
    s_out = pl.BlockSpec((None, ROWS, WIDTH), lambda i, j, r: ((i * N_PAIRS + j) * ATTN_PHASES + r, 0, 0))
    return pl.pallas_call(
        _attn_prompt_kernel,
        grid=(b, N_PAIRS, ATTN_PHASES),
        in_specs=[blk, blk, blk,
                  pl.BlockSpec((3, None, 2 * BLK, 2 * BLK), lambda i, j, r: (0, j, 0, 0))] + s_in,
        out_specs=[out_blk, s_out],
        out_shape=[jax.ShapeDtypeStruct((b, N_PAIRS, s, 128), BF16),
                   jax.ShapeDtypeStruct((n_rows, ROWS, WIDTH), F32)],
        scratch_shapes=[cls(()), cls(()), cls(()),
                        pltpu.VMEM((N_CLASS, 2 * BLK, BLK), F32),
                        pltpu.VMEM((n_wide, 2 * BLK, 2 * BLK), F32),
                        cls((2,)), cls(()), cls(()),
                        pltpu.VMEM((s, 128), F32)],
        compiler_params=pltpu.CompilerParams(
            dimension_semantics=("parallel", "parallel", "arbitrary"), vmem_limit_bytes=VMEM_LIMIT),
        name="attn_prompt",
    )(q_hp, k_hp, v_hp, bias_tab, *sample_ops)


def _new_token_mix_selectors(nb):
    n = nb * N_NEW
    t = np.arange(n) % N_NEW
    sel_w = np.zeros((N_NEW, N_NEW, N_NEW, n), np.float32)
    sel_b = np.zeros((N_NEW, n), np.float32)
    for r in range(n):
        sel_b[t[r], r] = 1.0
        for d in range(t[r] + 1):
            sel_w[d, t[r], t[r] - d, r] = 1.0
    return jnp.asarray(sel_w), jnp.asarray(sel_b)


def kernel(x_prompt, x_sample, cache_k, cache_v, norm_w, w_in, ln_v_w, ln_v_b, w_spatial, b_spatial,
           rel_bias, out_norm_a, out_norm_b, w_out, final_norm_w):
    b, s, _ = x_prompt.shape
    nb, nt, _ = x_sample.shape
    n_s = nb * nt
    rows_inproj = b * (s // INPROJ_TILE)
    assert cache_k.shape[2] == WB and nt == N_NEW and s % (N_CLASS * BLK) == 0
    assert rows_inproj + b * N_PAIRS * ATTN_PHASES == nb

    kt_all = jnp.transpose(cache_k, (0, 1, 3, 4, 2))
    vt_all = jnp.transpose(cache_v, (0, 1, 3, 4, 2))

    bias_prompt = _bias_tables(_prompt_bucket_index(), rel_bias).reshape(3, N_PAIRS, 2 * BLK, 2 * BLK)
    sample_idx, wtab = _sample_tables()
    btab = _bias_tables(sample_idx, rel_bias)[0]
    sel, new = _sample_selectors()
    sel_w, sel_b = _new_token_mix_selectors(nb)
    hi = lax.Precision.HIGHEST

    tril = np.tril(np.ones((CHUNK, CHUNK), np.float32))
    rows8 = lambda a: jnp.concatenate([a.reshape(nb, nt, WIDTH)] * (ROWS // nt), axis=1)

    xp = x_prompt
    xs = x_sample.reshape(1, n_s, D_MODEL)
    kt_stack = vt_stack = None
    sk, sv, sc = [], [], []
    win, wout = w_in.astype(BF16), w_out.astype(BF16)
    for l in range(DEPTH):
        common = (norm_w[l][None], win, ln_v_w[l][None], ln_v_b[l][None])
        mix_p = (w_spatial[l] * tril).astype(BF16)
        mixb_p = b_spatial[l][:, :, None]
        coef_s = jnp.einsum('gts,dtsr->gdr', w_spatial[l][:, :nt, :nt], sel_w, precision=hi)[..., None]
        mixb_s = jnp.einsum('gt,tr->gr', b_spatial[l][:, :nt], sel_b, precision=hi)[..., None]
        ga, gb = out_norm_a[l][None], out_norm_b[l][None]
        fw = final_norm_w[None]
        final = l == DEPTH - 1

        ya_s, q_s, k_s, v_s, vn_s, bz_s = _inproj_sample(xs[0], common + (coef_s, mixb_s, ga), l)
        sample_ops = (rows8(q_s), rows8(k_s), rows8(v_s), kt_all, vt_all, sel, new, btab, wtab)
        ya, q_hp, k_hp, v_hp, kt_stack, vt_stack, bz, o_t0 = _inproj_prompt(
            xp, common + (mix_p, mixb_p, ga), l, kt_stack, vt_stack, sample_ops)
        o_hp, o_t1 = _attn_prompt(q_hp, k_hp, v_hp, bias_prompt, sample_ops, l, rows_inproj)
        xp = _outproj(xp, ya, o_hp, bz, gb, wout, fw, l, final)

        o_s = jnp.concatenate([o_t0, o_t1], axis=0)[:, :nt, :].reshape(1, n_s, WIDTH)
        xs = _outproj(xs, ya_s[None], o_s, bz_s[None], gb, wout, fw, l, final)
        sk.append(k_s)
        sv.append(v_s)
        sc.append(vn_s)

    heads = (N_HEADS, HEAD_DIM)
    new_k_prompt = jnp.transpose(kt_stack, (0, 1, 4, 2, 3))
    new_v_prompt = jnp.transpose(vt_stack, (0, 1, 4, 2, 3))
    new_k_sample = jnp.stack(sk).reshape((DEPTH, nb, nt) + heads)
    new_v_sample = jnp.stack(sv).reshape((DEPTH, nb, nt) + heads)
    new_vchunk = jnp.stack(sc).reshape(DEPTH, nb, nt, WIDTH)
    return (xp, xs.reshape(nb, nt, D_MODEL), new_k_prompt, new_v_prompt,
            new_k_sample, new_v_sample, new_vchunk)
```
